```python
import math
import jax, jax.numpy as jnp
from jax import lax
import numpy as np

D_MODEL = 1024
BATCH = 32
SEQ = 256
DEPTH = 4
DEC_BATCH = 8
DEC_SEQ = 2048
PAST_LEN = 256

GRID_W = 64
N_MIXERS = 2
N_A_LAYERS = (DEPTH + 1) // 2
N_B_LAYERS = DEPTH // 2
A_HEAD_DIM = 64
A_HEADS = D_MODEL // A_HEAD_DIM
A_KV_HEADS = 4
B_HEAD_DIM = 64
B_HEADS = D_MODEL // (2 * B_HEAD_DIM)
N_EXPERTS = 16
N_GROUPS = 4
EXPERTS_PER_GROUP = N_EXPERTS // N_GROUPS
TOP_K = 2
D_EXPERT = 512
N_MOD = 6
ROPE_THETA = 10000.0
Q_BLOCK = 128
EPS = 1e-6

kernel_name = 'hybrid_diffusion_gqa_diffattn_grouped_moe_step'


def rms_norm(x, g):
    xf = x.astype(jnp.float32)
    y = xf * lax.rsqrt(jnp.mean(xf * xf, axis=-1, keepdims=True) + EPS)
    return (y * g.astype(jnp.float32)).astype(x.dtype)


def modulation(cvec, w, b):
    m = jax.nn.silu(cvec) @ w + b
    return jnp.split(m[:, None, :], N_MOD, axis=-1)


def modulate(h, shift, scale):
    return h * (1 + scale) + shift


def grid_angles(n, head_dim):
    rows = n // GRID_W
    row = jnp.repeat(jnp.arange(rows, dtype=jnp.float32), GRID_W)
    col = jnp.tile(jnp.arange(GRID_W, dtype=jnp.float32), rows)
    d_axis = head_dim // 2
    inv = ROPE_THETA ** (-jnp.arange(0, d_axis, 2, dtype=jnp.float32) / d_axis)
    return row[:, None] * inv, col[:, None] * inv


def _rotate(x, ang):
    x1, x2 = jnp.split(x, 2, axis=-1)
    c, s = jnp.cos(ang), jnp.sin(ang)
    return jnp.concatenate([x1 * c - x2 * s, x2 * c + x1 * s], axis=-1)


def axial_rope(x, angles):
    ang_row, ang_col = angles
    shape = (ang_row.shape[0],) + (1,) * (x.ndim - 3) + (ang_row.shape[1],)
    xf = x.astype(jnp.float32)
    xr, xc = jnp.split(xf, 2, axis=-1)
    out = jnp.concatenate([_rotate(xr, ang_row.reshape(shape)), _rotate(xc, ang_col.reshape(shape))], axis=-1)
    return out.astype(x.dtype)


def query_blocks(q):
    b, s = q.shape[:2]
    return jnp.moveaxis(q.reshape((b, s // Q_BLOCK, Q_BLOCK) + q.shape[2:]), 1, 0)


def unblock(o):
    o = jnp.moveaxis(o, 0, 1)
    return o.reshape((o.shape[0], o.shape[1] * o.shape[2]) + o.shape[3:])


def gqa_attention(q, k, v):
    b, s, h, d = q.shape
    g = k.shape[2]
    qg = q.reshape(b, s, g, h // g, d)
    scale = d ** -0.5

    def block(qb):
        sc = jnp.einsum('bqgrd,bkgd->bgrqk', qb, k, preferred_element_type=jnp.float32) * scale
        p = jax.nn.softmax(sc, axis=-1).astype(v.dtype)
        return jnp.einsum('bgrqk,bkgd->bqgrd', p, v)

    return unblock(lax.map(block, query_blocks(qg))).reshape(b, s, h * d)


def diff_attention(q, k, v, lam):
    scale = q.shape[-1] ** -0.5

    def block(qb):
        sc = jnp.einsum('bqhmd,bkhmd->bhmqk', qb, k, preferred_element_type=jnp.float32) * scale
        p = jax.nn.softmax(sc, axis=-1)
        a = (p[:, :, 0] - lam * p[:, :, 1]).astype(v.dtype)
        return jnp.einsum('bhqk,bkhe->bqhe', a, v)

    return unblock(lax.map(block, query_blocks(q)))


def a_project(h, w_qkv, g_q, g_k):
    b, n, _ = h.shape
    q, k, v = jnp.split(h @ w_qkv, [A_HEADS * A_HEAD_DIM, (A_HEADS + A_KV_HEADS) * A_HEAD_DIM], axis=-1)
    q = rms_norm(q.reshape(b, n, A_HEADS, A_HEAD_DIM), g_q)
    k = rms_norm(k.reshape(b, n, A_KV_HEADS, A_HEAD_DIM), g_k)
    return q, k, v.reshape(b, n, A_KV_HEADS, A_HEAD_DIM)


def b_project(h, w_qkv):
    b, n, _ = h.shape
    q, k, v = jnp.split(h @ w_qkv, 3, axis=-1)
    return (q.reshape(b, n, B_HEADS, 2, B_HEAD_DIM), k.reshape(b, n, B_HEADS, 2, B_HEAD_DIM),
            v.reshape(b, n, B_HEADS, 2 * B_HEAD_DIM))


def b_finish(o, g_sub, lam_init, w_o):
    b, s = o.shape[:2]
    o = rms_norm(o, g_sub) * (1.0 - lam_init)
    return o.reshape(b, s, D_MODEL) @ w_o


def grouped_moe(h, w_router, b_router, w_gate, w_up, w_down):
    b, n, d = h.shape
    t = h.reshape(b * n, d)
    s = jax.nn.sigmoid((t @ w_router + b_router).astype(jnp.float32))
    sg = s.reshape(-1, N_GROUPS, EXPERTS_PER_GROUP)
    group_score = jnp.sum(lax.top_k(sg, TOP_K)[0], axis=-1)
    best = jnp.argmax(group_score, axis=-1)
    in_group = jnp.arange(N_GROUPS)[None, :] == best[:, None]
    masked = jnp.where(in_group[:, :, None], sg, -jnp.inf).reshape(-1, N_EXPERTS)
    top_v, top_i = lax.top_k(masked, TOP_K)
    gates = top_v / jnp.sum(top_v, axis=-1, keepdims=True)
    combine = jnp.sum(jax.nn.one_hot(top_i, N_EXPERTS, dtype=jnp.float32) * gates[..., None], axis=1)
    out = jnp.zeros(t.shape, jnp.float32)
    for e in range(N_EXPERTS):
        he = jax.nn.silu(t @ w_gate[e]) * (t @ w_up[e])
        out = out + combine[:, e:e + 1] * (he @ w_down[e]).astype(jnp.float32)
    return out.astype(h.dtype).reshape(b, n, d)


def setup_inputs(seed: int = 0) -> dict:
    key = jax.random.key(seed)
    ks = jax.random.split(key, 32)
    f32 = jnp.float32
    nrm = lambda k, shape, s: jax.random.normal(k, shape, f32) * s
    D = D_MODEL
    return {
        'x_prompt': nrm(ks[0], (BATCH, SEQ, D), 1.0),
        'x_sample': nrm(ks[1], (DEC_BATCH, DEC_SEQ, D), 1.0),
        'cache_k_a': nrm(ks[2], (DEC_BATCH, N_A_LAYERS, PAST_LEN, A_KV_HEADS, A_HEAD_DIM), 1.0),
        'cache_v_a': nrm(ks[3], (DEC_BATCH, N_A_LAYERS, PAST_LEN, A_KV_HEADS, A_HEAD_DIM), 1.0),
        'cache_k_b': nrm(ks[4], (DEC_BATCH, N_B_LAYERS, PAST_LEN, B_HEADS, 2, B_HEAD_DIM), 1.0),
        'cache_v_b': nrm(ks[5], (DEC_BATCH, N_B_LAYERS, PAST_LEN, B_HEADS, 2 * B_HEAD_DIM), 1.0),
        'c': nrm(ks[6], (DEC_BATCH, D), 1.0),
        'c_ctx': nrm(ks[7], (D,), 1.0),
        'w_mod': nrm(ks[8], (DEPTH, D, N_MOD * D), 0.5 * D ** -0.5),
        'b_mod': nrm(ks[9], (DEPTH, N_MOD * D), 0.02),
        'g_attn': 1.0 + nrm(ks[10], (DEPTH, D), 0.02),
        'g_ffn': 1.0 + nrm(ks[11], (DEPTH, D), 0.02),
        'w_qkv_a': nrm(ks[12], (N_A_LAYERS, D, (A_HEADS + 2 * A_KV_HEADS) * A_HEAD_DIM), D ** -0.5),
        'w_o_a': nrm(ks[13], (N_A_LAYERS, A_HEADS * A_HEAD_DIM, D), (A_HEADS * A_HEAD_DIM) ** -0.5),
        'g_q_a': 1.0 + nrm(ks[14], (N_A_LAYERS, A_HEAD_DIM), 0.02),
        'g_k_a': 1.0 + nrm(ks[15], (N_A_LAYERS, A_HEAD_DIM), 0.02),
        'w_qkv_b': nrm(ks[16], (N_B_LAYERS, D, 3 * D), D ** -0.5),
        'w_o_b': nrm(ks[17], (N_B_LAYERS, D, D), D ** -0.5),
        'lam_q1': nrm(ks[18], (N_B_LAYERS, B_HEAD_DIM), 0.1),
        'lam_k1': nrm(ks[19], (N_B_LAYERS, B_HEAD_DIM), 0.1),
        'lam_q2': nrm(ks[20], (N_B_LAYERS, B_HEAD_DIM), 0.1),
        'lam_k2': nrm(ks[21], (N_B_LAYERS, B_HEAD_DIM), 0.1),
        'g_sub_b': 1.0 + nrm(ks[22], (N_B_LAYERS, 2 * B_HEAD_DIM), 0.02),
        'w_router': nrm(ks[23], (D, N_EXPERTS), D ** -0.5),
        'b_router': nrm(ks[24], (N_EXPERTS,), 0.01),
        'w_gate': nrm(ks[25], (DEPTH, N_EXPERTS, D, D_EXPERT), D ** -0.5),
        'w_up': nrm(ks[26], (DEPTH, N_EXPERTS, D, D_EXPERT), D ** -0.5),
        'w_down': nrm(ks[27], (DEPTH, N_EXPERTS, D_EXPERT, D), D_EXPERT ** -0.5),
        'g_final': 1.0 + nrm(ks[28], (D,), 0.02),
    }


def reference(x_prompt, x_sample, cache_k_a, cache_v_a, cache_k_b, cache_v_b, c, c_ctx,
              w_mod, b_mod, g_attn, g_ffn, w_qkv_a, w_o_a, g_q_a, g_k_a,
              w_qkv_b, w_o_b, lam_q1, lam_k1, lam_q2, lam_k2, g_sub_b,
              w_router, b_router, w_gate, w_up, w_down, g_final):
    rope_a = grid_angles(x_sample.shape[1], A_HEAD_DIM)
    rope_b = grid_angles(x_sample.shape[1], B_HEAD_DIM)
    xp, xs = x_prompt, x_sample
    new_k_a, new_v_a, new_k_b, new_v_b = [], [], [], []
    for l in range(DEPTH):
        mp = modulation(c_ctx[None, :], w_mod[l], b_mod[l])
        ms = modulation(c, w_mod[l], b_mod[l])
        hp = modulate(rms_norm(xp, g_attn[l]), mp[0], mp[1])
        hs = modulate(rms_norm(xs, g_attn[l]), ms[0], ms[1])
        j = l // N_MIXERS
        if l % N_MIXERS == 0:
            qp, kp, vp = a_project(hp, w_qkv_a[j], g_q_a[j], g_k_a[j])
            out_p = gqa_attention(qp, kp, vp) @ w_o_a[j]
            qs, ks_, vs = a_project(hs, w_qkv_a[j], g_q_a[j], g_k_a[j])
            qs, ks_ = axial_rope(qs, rope_a), axial_rope(ks_, rope_a)
            k_all = jnp.concatenate([ks_, cache_k_a[:, j]], axis=1)
            v_all = jnp.concatenate([vs, cache_v_a[:, j]], axis=1)
            out_s = gqa_attention(qs, k_all, v_all) @ w_o_a[j]
            new_k_a.append(kp)
            new_v_a.append(vp)
        else:
            lam_init = 0.8 - 0.6 * math.exp(-0.3 * l)
            lam = (jnp.exp(jnp.sum((lam_q1[j] * lam_k1[j]).astype(jnp.float32)))
                   - jnp.exp(jnp.sum((lam_q2[j] * lam_k2[j]).astype(jnp.float32))) + lam_init)
            qp, kp, vp = b_project(hp, w_qkv_b[j])
            out_p = b_finish(diff_attention(qp, kp, vp, lam), g_sub_b[j], lam_init, w_o_b[j])
            qs, ks_, vs = b_project(hs, w_qkv_b[j])
            qs, ks_ = axial_rope(qs, rope_b), axial_rope(ks_, rope_b)
            k_all = jnp.concatenate([ks_, cache_k_b[:, j]], axis=1)
            v_all = jnp.concatenate([vs, cache_v_b[:, j]], axis=1)
            out_s = b_finish(diff_attention(qs, k_all, v_all, lam), g_sub_b[j], lam_init, w_o_b[j])
            new_k_b.append(kp)
            new_v_b.append(vp)
        xp = xp + mp[2] * out_p
        xs = xs + ms[2] * out_s
        hp = modulate(rms_norm(xp, g_ffn[l]), mp[3], mp[4])
        hs = modulate(rms_norm(xs, g_ffn[l]), ms[3], ms[4])
        xp = xp + mp[5] * grouped_moe(hp, w_router, b_router, w_gate[l], w_up[l], w_down[l])
        xs = xs + ms[5] * grouped_moe(hs, w_router, b_router, w_gate[l], w_up[l], w_down[l])
    y_prompt = rms_norm(xp, g_final)
    y_sample = rms_norm(xs, g_final)
    state_k_a = jnp.stack(new_k_a, axis=1)
    state_v_a = jnp.stack(new_v_a, axis=1)
    state_k_b = jnp.stack(new_k_b, axis=1)
    state_v_b = jnp.stack(new_v_b, axis=1)
    return (y_prompt, y_sample, state_k_a, state_v_a, state_k_b, state_v_b)
```

```python
import functools
import math

import jax
import jax.numpy as jnp
from jax import lax
from jax.experimental import pallas as pl
from jax.experimental.pallas import tpu as pltpu

F32 = jnp.float32
BF16 = jnp.bfloat16

D_MODEL = 1024
HEAD_DIM = 64
A_HEADS = 16
A_KV_HEADS = 4
B_HEADS = 8
N_EXPERTS = 16
N_GROUPS = 4
EXPERTS_PER_GROUP = 4
D_EXPERT = 512
N_MOD = 6
N_MOD_ROWS = 16
ROPE_THETA = 10000.0
GRID_W = 64
EPS = 1e-6

LANES = 128
N_CHUNKS = D_MODEL // LANES
TM = 256
TQ_A = 128
TQ_B = 256
R_MOE = 256
N_PAIRS = 6
N_BUCKETS = N_GROUPS * N_PAIRS
PAIR_A = (0, 0, 0, 1, 1, 2)
PAIR_B = (1, 2, 3, 2, 3, 3)
VMEM_LIMIT = 56 * 1024 * 1024


def _cparams(n_axes=1):
    return pltpu.CompilerParams(dimension_semantics=("arbitrary",) * n_axes,
                                vmem_limit_bytes=VMEM_LIMIT)


def _slab_to_nat(ref, tm):
    return jnp.concatenate([ref[pl.ds(c, tm, stride=N_CHUNKS), :] for c in range(N_CHUNKS)], axis=-1)


def _nat_to_slab(ref, val, tm):
    for c in range(N_CHUNKS):
        ref[pl.ds(c, tm, stride=N_CHUNKS), :] = val[:, LANES * c:LANES * (c + 1)]


def _norm_mod(x, g, shift, scale):
    ms = jnp.mean(x * x, axis=-1, keepdims=True)
    return x * lax.rsqrt(ms + EPS) * g * (1.0 + scale) + shift


def _rope_cols(x, cos, sin, second_of_pair):
    outs = []
    for c in range(x.shape[1] // LANES):
        xc = x[:, LANES * c:LANES * (c + 1)]
        partner = jnp.where(second_of_pair, pltpu.roll(xc, 16, 1), pltpu.roll(xc, LANES - 16, 1))
        outs.append(xc * cos + partner * sin)
    return jnp.concatenate(outs, axis=-1)


def _stream_x(i, npt, refs, has_prev, tm):
    if has_prev:
        x1_ref, y_ref, g2_ref = refs
        return x1_ref[...] + g2_ref[0] * _slab_to_nat(y_ref, tm)
    xp_ref, xs_ref = refs
    return jnp.where(i < npt, xp_ref[...], xs_ref[...])


def _mod_kernel(cv_ref, w_ref, b_ref, o_ref):
    cv = cv_ref[...]
    s = cv * (1.0 / (1.0 + jnp.exp(-cv)))
    o_ref[0] = jnp.dot(s.astype(BF16), w_ref[0].astype(BF16), preferred_element_type=F32) + b_ref[0]


def _modulation(cvec, w_mod, b_mod):
    depth, d, n = w_mod.shape
    tn = 1024
    return pl.pallas_call(
        _mod_kernel,
        grid=(depth, n // tn),
        in_specs=[pl.BlockSpec((N_MOD_ROWS, d), lambda l, j: (0, 0)),
                  pl.BlockSpec((1, d, tn), lambda l, j: (l, 0, j)),
                  pl.BlockSpec((1, 1, tn), lambda l, j: (l, 0, j))],
        out_specs=pl.BlockSpec((1, N_MOD_ROWS, tn), lambda l, j: (l, 0, j)),
        out_shape=jax.ShapeDtypeStruct((depth, N_MOD_ROWS, n), F32),
        compiler_params=_cparams(2),
        name="modulation",
    )(cvec, w_mod, b_mod.reshape(depth, 1, n))


def _proj_a_kernel(has_prev, npt, *refs):
    n_x = 3 if has_prev else 2
    xrefs, refs = refs[:n_x], refs[n_x:]
    (sh_ref, sc_ref, ga_ref, w_ref, eq_ref, et_ref, gv_ref, cos_ref, sin_ref,
     q_ref, k_ref, v_ref, ks_ref, vs_ref) = refs
    i = pl.program_id(0)
    x = _stream_x(i, npt, xrefs, has_prev, TM)
    hb = _norm_mod(x, ga_ref[...], sh_ref[0], sc_ref[0]).astype(BF16)
    qkv = jnp.dot(hb, w_ref[...], preferred_element_type=F32)
    nqk = D_MODEL + 2 * A_KV_HEADS * HEAD_DIM
    qk = qkv[:, :nqk]
    ms = jnp.dot((qk * qk).astype(BF16), eq_ref[...], preferred_element_type=F32)
    inv = lax.rsqrt(ms + EPS)
    inv_hi = inv.astype(BF16)
    inv_lo = (inv - inv_hi.astype(F32)).astype(BF16)
    invf = jnp.dot(jnp.concatenate([inv_hi, inv_lo], axis=-1), et_ref[...], preferred_element_type=F32)
    qkn = qk * invf * gv_ref[...]
    lane = lax.broadcasted_iota(jnp.int32, (1, LANES), 1)
    second = (lane & 16) != 0
    lo = lane < HEAD_DIM
    qkr = _rope_cols(qkn, cos_ref[...], sin_ref[...], second)
    q_ref[...] = qkr[:, :D_MODEL].astype(BF16)
    k_ref[...] = qkr[:, D_MODEL:].astype(BF16)
    vd = qkv[:, nqk:]
    v_ref[...] = vd.astype(BF16)

    @pl.when(i < npt)
    def _():
        kd = qkn[:, D_MODEL:]
        for j in range(2):
            ks_ref[:, LANES * j:LANES * (j + 1)] = jnp.where(
                lo, kd[:, LANES * 2 * j:LANES * (2 * j + 1)], kd[:, LANES * (2 * j + 1):LANES * (2 * j + 2)])
            vs_ref[:, LANES * j:LANES * (j + 1)] = jnp.where(
                lo, vd[:, LANES * 2 * j:LANES * (2 * j + 1)], vd[:, LANES * (2 * j + 1):LANES * (2 * j + 2)])


def _proj_b_kernel(has_prev, npt, *refs):
    n_x = 3 if has_prev else 2
    xrefs, refs = refs[:n_x], refs[n_x:]
    (sh_ref, sc_ref, ga_ref, w_ref, cos_ref, sin_ref,
     q_ref, k_ref, v_ref, ks_ref, vs_ref) = refs
    i = pl.program_id(0)
    x = _stream_x(i, npt, xrefs, has_prev, TM)
    hb = _norm_mod(x, ga_ref[...], sh_ref[0], sc_ref[0]).astype(BF16)
    qkv = jnp.dot(hb, w_ref[...], preferred_element_type=F32)
    lane = lax.broadcasted_iota(jnp.int32, (1, LANES), 1)
    second = (lane & 16) != 0
    qk = qkv[:, :2 * D_MODEL]
    qkr = _rope_cols(qk, cos_ref[...], sin_ref[...], second)
    q_ref[...] = qkr[:, :D_MODEL].astype(BF16)
    k_ref[...] = qkr[:, D_MODEL:].astype(BF16)
    v = qkv[:, 2 * D_MODEL:]
    v_ref[...] = v.astype(BF16)

    @pl.when(i < npt)
    def _():
        ks_ref[...] = qk[:, D_MODEL:]
        vs_ref[...] = v


def _x_specs(has_prev, npt):
    if has_prev:
        return [pl.BlockSpec((TM, D_MODEL), lambda i: (i, 0)),
                pl.BlockSpec((TM * N_CHUNKS, LANES), lambda i: (i, 0)),
                None]
    return [pl.BlockSpec((TM, D_MODEL), lambda i: (jnp.minimum(i, npt - 1), 0)),
            pl.BlockSpec((TM, D_MODEL), lambda i: (jnp.maximum(i - npt, 0), 0))]


def _mod_spec(layer, k, npt, tpb):
    def index(i):
        row = jnp.where(i < npt, 0, 1 + (i - npt) // tpb)
        return ((layer * N_MOD_ROWS + row) * N_MOD + k, 0, 0)
    return pl.BlockSpec((1, 1, D_MODEL), index)


def _rope_spec(npt, tpb):
    return pl.BlockSpec((TM, LANES), lambda i: (jnp.where(i < npt, 0, 1 + (i - npt) % tpb), 0))


def _const_spec(shape):
    nd = len(shape)
    return pl.BlockSpec(shape, lambda i: (0,) * nd)


def _lane_masks():
    lane = lax.broadcasted_iota(jnp.int32, (1, LANES), 1)
    lo = lane < HEAD_DIM
    mlo = jnp.where(lo, 1.0, 0.0).astype(BF16)
    mhi = jnp.where(lo, 0.0, 1.0).astype(BF16)
    return lo, mlo, mhi


_NT_DIMS = (((1,), (1,)), ((), ()))


def _attn_a_kernel(tq, has_cache, *refs):
    if has_cache:
        q_ref, k_ref, v_ref, ck_ref, cv_ref, o_ref = refs
    else:
        q_ref, k_ref, v_ref, o_ref = refs
    lo, mlo, mhi = _lane_masks()
    for g in range(A_KV_HEADS):
        cols = slice(LANES * g, LANES * (g + 1))
        qa = q_ref[:, LANES * 2 * g:LANES * (2 * g + 1)]
        qb = q_ref[:, LANES * (2 * g + 1):LANES * (2 * g + 2)]
        lhs = jnp.concatenate([qa * mlo, qa * mhi, qb * mlo, qb * mhi], axis=0)
        s = lax.dot_general(lhs, k_ref[:, cols], _NT_DIMS, preferred_element_type=F32)
        m = jnp.max(s, axis=-1, keepdims=True)
        if has_cache:
            s2 = lax.dot_general(lhs, ck_ref[0, 0, :, cols], _NT_DIMS, preferred_element_type=F32)
            m = jnp.maximum(m, jnp.max(s2, axis=-1, keepdims=True))
        p = jnp.exp(s - m)
        l = jnp.sum(p, axis=-1, keepdims=True)
        o = jnp.dot(p.astype(BF16), v_ref[:, cols], preferred_element_type=F32)
        if has_cache:
            p2 = jnp.exp(s2 - m)
            l = l + jnp.sum(p2, axis=-1, keepdims=True)
            o = o + jnp.dot(p2.astype(BF16), cv_ref[0, 0, :, cols], preferred_element_type=F32)
        o = o * (1.0 / l)
        o_ref[:, LANES * 2 * g:LANES * (2 * g + 1)] = jnp.where(lo, o[0:tq], o[tq:2 * tq]).astype(BF16)
        o_ref[:, LANES * (2 * g + 1):LANES * (2 * g + 2)] = jnp.where(
            lo, o[2 * tq:3 * tq], o[3 * tq:4 * tq]).astype(BF16)


def _attn_b_kernel(tq, has_cache, *refs):
    if has_cache:
        lam_ref, q_ref, k_ref, v_ref, ck_ref, cv_ref, gs_ref, o_ref = refs
    else:
        lam_ref, q_ref, k_ref, v_ref, gs_ref, o_ref = refs
    lam = lam_ref[0, 0]
    _, mlo, mhi = _lane_masks()
    for h in range(B_HEADS):
        cols = slice(LANES * h, LANES * (h + 1))
        qc = q_ref[:, cols]
        lhs = jnp.concatenate([qc * mlo, qc * mhi], axis=0)
        s = lax.dot_general(lhs, k_ref[:, cols], _NT_DIMS, preferred_element_type=F32)
        m = jnp.max(s, axis=-1, keepdims=True)
        if has_cache:
            s2 = lax.dot_general(lhs, ck_ref[0, 0, :, cols], _NT_DIMS, preferred_element_type=F32)
            m = jnp.maximum(m, jnp.max(s2, axis=-1, keepdims=True))
        p = jnp.exp(s - m)
        l = jnp.sum(p, axis=-1, keepdims=True)
        if has_cache:
            p2 = jnp.exp(s2 - m)
            l = l + jnp.sum(p2, axis=-1, keepdims=True)
        r = 1.0 / l
        r0 = r[0:tq]
        r1 = lam * r[tq:2 * tq]
        a = p[0:tq] * r0 - p[tq:2 * tq] * r1
        o = jnp.dot(a.astype(BF16), v_ref[:, cols], preferred_element_type=F32)
        if has_cache:
            a2 = p2[0:tq] * r0 - p2[tq:2 * tq] * r1
            o = o + jnp.dot(a2.astype(BF16), cv_ref[0, 0, :, cols], preferred_element_type=F32)
        ms = jnp.mean(o * o, axis=-1, keepdims=True)
        o_ref[:, cols] = (o * lax.rsqrt(ms + EPS) * gs_ref[...]).astype(BF16)


def _attention(kind, q, k, v, nbp, seq, nbs, dseq, cache_k, cache_v, layer_j, lam=None, gsub=None):
    t_all = q.shape[0]
    wk = k.shape[1]
    tq = TQ_A if kind == "a" else TQ_B
    kern = _attn_a_kernel if kind == "a" else _attn_b_kernel
    pre_args, pre_specs_1, pre_specs_2 = [], [], []
    post_args, post_specs_1, post_specs_2 = [], [], []
    if kind == "b":
        pre_args = [lam]
        pre_specs_1 = [pl.BlockSpec(memory_space=pltpu.SMEM)]
        pre_specs_2 = [pl.BlockSpec(memory_space=pltpu.SMEM)]
        post_args = [gsub]
        post_specs_1 = [pl.BlockSpec((1, LANES), lambda b: (0, 0))]
        post_specs_2 = [pl.BlockSpec((1, LANES), lambda b, j: (0, 0))]

    tqp = min(tq, seq)
    nqp = seq // tqp
    out_p = pl.pallas_call(
        functools.partial(kern, tqp, False),
        grid=(nbp * nqp,),
        in_specs=pre_specs_1 + [
            pl.BlockSpec((tqp, D_MODEL), lambda b: (b, 0)),
            pl.BlockSpec((seq, wk), lambda b: (b // nqp, 0)),
            pl.BlockSpec((seq, wk), lambda b: (b // nqp, 0))] + post_specs_1,
        out_specs=pl.BlockSpec((tqp, D_MODEL), lambda b: (b, 0)),
        out_shape=jax.ShapeDtypeStruct((nbp * seq, D_MODEL), BF16),
        compiler_params=_cparams(1),
        name="attn_%s_ctx" % kind,
    )(*pre_args, q, k, v, *post_args)

    nq = dseq // tq
    qoff = (nbp * seq) // tq
    koff = (nbp * seq) // dseq
    assert (nbp * seq) % dseq == 0
    past = cache_k.shape[2]
    out_s = pl.pallas_call(
        functools.partial(kern, tq, True),
        grid=(nbs, nq),
        in_specs=pre_specs_2 + [
            pl.BlockSpec((tq, D_MODEL), lambda b, j: (qoff + b * nq + j, 0)),
            pl.BlockSpec((dseq, wk), lambda b, j: (koff + b, 0)),
            pl.BlockSpec((dseq, wk), lambda b, j: (koff + b, 0)),
            pl.BlockSpec((1, 1, past, wk), lambda b, j: (b, layer_j, 0, 0)),
            pl.BlockSpec((1, 1, past, wk), lambda b, j: (b, layer_j, 0, 0))] + post_specs_2,
        out_specs=pl.BlockSpec((tq, D_MODEL), lambda b, j: (b * nq + j, 0)),
        out_shape=jax.ShapeDtypeStruct((nbs * dseq, D_MODEL), BF16),
        compiler_params=_cparams(2),
        name="attn_%s_lat" % kind,
    )(*pre_args, q, k, v, cache_k, cache_v, *post_args)
    del t_all
    return out_p, out_s


def _first_max_index(vals, m):
    idx = jnp.full_like(m, float(len(vals) - 1))
    for j in range(len(vals) - 2, -1, -1):
        idx = jnp.where(vals[j] == m, float(j), idx)
    return idx


def _select(idx, vals):
    out = vals[-1]
    for j in range(len(vals) - 2, -1, -1):
        out = jnp.where(idx == float(j), vals[j], out)
    return out


def _kb_kernel(has_prev, npt, *refs):
    n_x = 3 if has_prev else 2
    xrefs, refs = refs[:n_x], refs[n_x:]
    (op_ref, os_ref, g1_ref, sh_ref, sc_ref, wo_ref, gf_ref, wrh_ref, wrl_ref, br_ref,
     x1_ref, h2_ref, rt_ref, lt_scr) = refs
    i = pl.program_id(0)
    x = _stream_x(i, npt, xrefs, has_prev, TM)
    o = jnp.where(i < npt, op_ref[...], os_ref[...])
    a = jnp.dot(o, wo_ref[...], preferred_element_type=F32)
    x1 = x + g1_ref[0] * a
    x1_ref[...] = x1
    h2 = _norm_mod(x1, gf_ref[...], sh_ref[0], sc_ref[0])
    _nat_to_slab(h2_ref, h2, TM)
    h_hi = h2.astype(BF16)
    h_lo = (h2 - h_hi.astype(F32)).astype(BF16)
    logits = (jnp.dot(h_hi, wrh_ref[...], preferred_element_type=F32)
              + jnp.dot(h_hi, wrl_ref[...], preferred_element_type=F32)
              + jnp.dot(h_lo, wrh_ref[...], preferred_element_type=F32)
              + br_ref[...])
    lt_scr[...] = logits.T
    sc = []
    for e in range(N_EXPERTS):
        z = lt_scr[e:e + 1, :]
        sc.append(1.0 / (1.0 + jnp.exp(-z)))
    ninf = jnp.float32(-jnp.inf)
    gscore, gtop = [], []
    for g in range(N_GROUPS):
        a4 = sc[EXPERTS_PER_GROUP * g:EXPERTS_PER_GROUP * (g + 1)]
        m1 = jnp.maximum(jnp.maximum(a4[0], a4[1]), jnp.maximum(a4[2], a4[3]))
        i1 = _first_max_index(a4, m1)
        b4 = [jnp.where(i1 == float(j), ninf, a4[j]) for j in range(EXPERTS_PER_GROUP)]
        m2 = jnp.maximum(jnp.maximum(b4[0], b4[1]), jnp.maximum(b4[2], b4[3]))
        i2 = _first_max_index(b4, m2)
        gscore.append(m1 + m2)
        gtop.append((m1, i1, m2, i2))
    gm = jnp.maximum(jnp.maximum(gscore[0], gscore[1]), jnp.maximum(gscore[2], gscore[3]))
    best = _first_max_index(gscore, gm)
    m1 = _select(best, [t[0] for t in gtop])
    i1 = _select(best, [t[1] for t in gtop])
    m2 = _select(best, [t[2] for t in gtop])
    i2 = _select(best, [t[3] for t in gtop])
    den = m1 + m2
    first_low = i1 < i2
    ia = jnp.minimum(i1, i2)
    ib = jnp.maximum(i1, i2)
    pid = jnp.where(ia == 0.0, ib - 1.0, jnp.where(ia == 1.0, ib + 1.0, 5.0))
    rt_ref[...] = jnp.zeros(rt_ref.shape, F32)
    rt_ref[0:1, :] = best * float(N_PAIRS) + pid
    rt_ref[1:2, :] = jnp.where(first_low, m1, m2) / den
    rt_ref[2:3, :] = jnp.where(first_low, m2, m1) / den


def _moe_kernel(r_tile, n_tok, ea_ref, eb_ref, nv_ref,
                gcur_ref, gnext_ref, scur_ref, gates_ref, h2_hbm,
                wga_ref, wua_ref, wda_ref, wgb_ref, wub_ref, wdb_ref,
                y_hbm, xbuf, ybuf, gsem, ssem, zsem):
    t = pl.program_id(0)
    nvalid = nv_ref[0]
    slot = t % 2

    def gather_start(idx_ref, s):
        for r in range(r_tile):
            pltpu.make_async_copy(h2_hbm.at[idx_ref[0, 0, r]],
                                  xbuf.at[s, pl.ds(N_CHUNKS * r, N_CHUNKS)], gsem.at[s]).start()

    def gather_wait(s):
        pltpu.make_async_copy(xbuf.at[s], xbuf.at[s], gsem.at[s]).wait()

    def scatter_wait():
        pltpu.make_async_copy(ybuf, ybuf, ssem).wait()

    @pl.when(t == 0)
    def _():
        ybuf[...] = jnp.zeros(ybuf.shape, F32)
        for r in range(r_tile):
            pltpu.make_async_copy(ybuf.at[pl.ds(N_CHUNKS * r, N_CHUNKS)], y_hbm.at[n_tok + r], zsem).start()
        pltpu.make_async_copy(ybuf, ybuf, zsem).wait()
        gather_start(gcur_ref, 0)

    @pl.when(t < nvalid)
    def _():
        gather_start(gnext_ref, 1 - slot)
        gather_wait(slot)
        x = _slab_to_nat(xbuf.at[slot], r_tile).astype(BF16)
        gts = gates_ref[...]
        y = None
        for (wg, wu, wd, col) in ((wga_ref, wua_ref, wda_ref, 0), (wgb_ref, wub_ref, wdb_ref, 1)):
            g = jnp.dot(x, wg[0], preferred_element_type=F32)
            u = jnp.dot(x, wu[0], preferred_element_type=F32)
            hmid = (g * (1.0 / (1.0 + jnp.exp(-g))) * u).astype(BF16)
            ye = gts[:, col:col + 1] * jnp.dot(hmid, wd[0], preferred_element_type=F32)
            y = ye if y is None else y + ye

        @pl.when(t > 0)
        def _():
            scatter_wait()

        _nat_to_slab(ybuf, y, r_tile)
        for r in range(r_tile):
            pltpu.make_async_copy(ybuf.at[pl.ds(N_CHUNKS * r, N_CHUNKS)],
                                  y_hbm.at[scur_ref[0, 0, r]], ssem).start()

        @pl.when(t == nvalid - 1)
        def _():
            gather_wait(1 - slot)
            scatter_wait()


def _moe(h2_slab, rt, wg, wu, wd, n_tok):
    r = R_MOE
    nt = n_tok // r + N_BUCKETS
    bucket = rt[0].astype(jnp.int32)
    order = jnp.argsort(bucket, stable=True).astype(jnp.int32)
    onehot = (bucket[:, None] == jnp.arange(N_BUCKETS, dtype=jnp.int32)[None, :]).astype(jnp.int32)
    counts = jnp.sum(onehot, axis=0)
    tiles_b = (counts + r - 1) // r
    tile_end = jnp.cumsum(tiles_b)
    tile_start = tile_end - tiles_b
    nvalid = tile_end[-1]
    first = jnp.cumsum(counts) - counts
    sb = bucket[order]
    dest = tile_start[sb] * r + (jnp.arange(n_tok, dtype=jnp.int32) - first[sb])
    slots = jnp.arange(nt * r, dtype=jnp.int32)
    gidx = jnp.zeros((nt * r,), jnp.int32).at[dest].set(order)
    sidx = (n_tok + slots % r).at[dest].set(order)
    gates = jnp.zeros((nt * r, 2), F32).at[dest].set(jnp.stack([rt[1], rt[2]], axis=1)[order])
    tix = jnp.minimum(jnp.arange(nt, dtype=jnp.int32), nvalid - 1)
    tb = jnp.minimum(jnp.sum((tix[:, None] >= tile_end[None, :]).astype(jnp.int32), axis=1), N_BUCKETS - 1)
    grp = tb // N_PAIRS
    ea = (grp * EXPERTS_PER_GROUP + jnp.asarray(PAIR_A, jnp.int32)[tb % N_PAIRS]).astype(jnp.int32)
    eb = (grp * EXPERTS_PER_GROUP + jnp.asarray(PAIR_B, jnp.int32)[tb % N_PAIRS]).astype(jnp.int32)
    nv = nvalid.reshape(1).astype(jnp.int32)
    gidx3 = gidx.reshape(nt, 1, r)
    sidx3 = sidx.reshape(nt, 1, r)

    smem_cur = pl.BlockSpec((1, 1, r), lambda t, ea, eb, nv: (t, 0, 0), memory_space=pltpu.SMEM)
    smem_next = pl.BlockSpec((1, 1, r), lambda t, ea, eb, nv: (jnp.minimum(t + 1, nt - 1), 0, 0),
                             memory_space=pltpu.SMEM)
    w_in_a = pl.BlockSpec((1, D_MODEL, D_EXPERT), lambda t, ea, eb, nv: (ea[t], 0, 0))
    w_out_a = pl.BlockSpec((1, D_EXPERT, D_MODEL), lambda t, ea, eb, nv: (ea[t], 0, 0))
    w_in_b = pl.BlockSpec((1, D_MODEL, D_EXPERT), lambda t, ea, eb, nv: (eb[t], 0, 0))
    w_out_b = pl.BlockSpec((1, D_EXPERT, D_MODEL), lambda t, ea, eb, nv: (eb[t], 0, 0))
    grid_spec = pltpu.PrefetchScalarGridSpec(
        num_scalar_prefetch=3,
        grid=(nt,),
        in_specs=[smem_cur, smem_next, smem_cur,
                  pl.BlockSpec((r, 2), lambda t, ea, eb, nv: (t, 0)),
                  pl.BlockSpec(memory_space=pl.ANY),
                  w_in_a, w_in_a, w_out_a, w_in_b, w_in_b, w_out_b],
        out_specs=pl.BlockSpec(memory_space=pl.ANY),
        scratch_shapes=[pltpu.VMEM((2, r * N_CHUNKS, LANES), F32),
                        pltpu.VMEM((r * N_CHUNKS, LANES), F32),
                        pltpu.SemaphoreType.DMA((2,)),
                        pltpu.SemaphoreType.DMA,
                        pltpu.SemaphoreType.DMA],
    )
    y = pl.pallas_call(
        functools.partial(_moe_kernel, r, n_tok),
        grid_spec=grid_spec,
        out_shape=jax.ShapeDtypeStruct((n_tok + r, N_CHUNKS, LANES), F32),
        compiler_params=_cparams(1),
        name="moe_experts",
    )(ea, eb, nv, gidx3, gidx3, sidx3, gates, h2_slab, wg, wu, wd, wg, wu, wd)
    return y


def _final_kernel(npt, x1_ref, y_ref, g2_ref, gf_ref, op_ref, os_ref):
    i = pl.program_id(0)
    x = x1_ref[...] + g2_ref[0] * _slab_to_nat(y_ref, TM)
    ms = jnp.mean(x * x, axis=-1, keepdims=True)
    out = x * lax.rsqrt(ms + EPS) * gf_ref[...]

    @pl.when(i < npt)
    def _():
        op_ref[...] = out

    @pl.when(i >= npt)
    def _():
        os_ref[...] = out


def _rope_tables(dseq):
    n_freq = HEAD_DIM // 4
    inv = ROPE_THETA ** (-jnp.arange(0, 2 * n_freq, 2, dtype=F32) / (2 * n_freq))
    pos = jnp.arange(dseq)
    row = (pos // GRID_W).astype(F32)
    col = (pos % GRID_W).astype(F32)
    ang_r = row[:, None] * inv
    ang_c = col[:, None] * inv
    ang = jnp.concatenate([ang_r, ang_r, ang_c, ang_c], axis=-1)
    sign = jnp.tile(jnp.concatenate([-jnp.ones((n_freq,), F32), jnp.ones((n_freq,), F32)]), 2)
    cos = jnp.tile(jnp.cos(ang), (1, 2))
    sin = jnp.tile(jnp.sin(ang) * sign, (1, 2))
    cos = jnp.concatenate([jnp.ones((TM, LANES), F32), cos], axis=0)
    sin = jnp.concatenate([jnp.zeros((TM, LANES), F32), sin], axis=0)
    return cos, sin


def _dup_halves(w, n_heads):
    lead = w.shape[:-1]
    w = w.reshape(lead + (n_heads, 1, HEAD_DIM))
    w = jnp.broadcast_to(w, lead + (n_heads, 2, HEAD_DIM))
    return w.reshape(lead + (n_heads * 2 * HEAD_DIM,))


def kernel(x_prompt, x_sample, cache_k_a, cache_v_a, cache_k_b, cache_v_b, c, c_ctx, w_mod, b_mod, g_attn, g_ffn, w_qkv_a, w_o_a, g_q_a, g_k_a, w_qkv_b, w_o_b, lam_q1, lam_k1, lam_q2, lam_k2, g_sub_b, w_router, b_router, w_gate, w_up, w_down, g_final):
    nbp, seq, d = x_prompt.shape
    nbs, dseq, _ = x_sample.shape
    depth = w_mod.shape[0]
    assert d == D_MODEL and seq % TM == 0 and dseq % TM == 0
    tp, ts = nbp * seq, nbs * dseq
    t_all = tp + ts
    npt = tp // TM
    tpb = dseq // TM
    nti = t_all // TM
    assert nbs + 1 <= N_MOD_ROWS

    xp = x_prompt.reshape(tp, d)
    xs = x_sample.reshape(ts, d)

    cvec = jnp.zeros((N_MOD_ROWS, d), F32).at[0].set(c_ctx).at[1:1 + nbs].set(c)
    mod = _modulation(cvec, w_mod, b_mod).reshape(depth * N_MOD_ROWS * N_MOD, 1, d)

    cos_t, sin_t = _rope_tables(dseq)
    qscale = HEAD_DIM ** -0.5

    nqk = D_MODEL + 2 * A_KV_HEADS * HEAD_DIM
    lane_id = jnp.arange(nqk)
    seg = jnp.where(lane_id < D_MODEL, lane_id // HEAD_DIM, A_HEADS + (lane_id - D_MODEL) // (2 * HEAD_DIM))
    seg_w = jnp.where(lane_id < D_MODEL, 1.0 / HEAD_DIM, 1.0 / (2 * HEAD_DIM))
    onehot = (seg[:, None] == jnp.arange(LANES)[None, :]).astype(F32)
    eq = (onehot * seg_w[:, None]).astype(BF16)
    et = jnp.concatenate([onehot.T, onehot.T], axis=0).astype(BF16)

    w_router_p = jnp.zeros((d, LANES), F32).at[:, :N_EXPERTS].set(w_router)
    wr_hi = w_router_p.astype(BF16)
    wr_lo = (w_router_p - wr_hi.astype(F32)).astype(BF16)
    br = jnp.zeros((1, LANES), F32).at[0, :N_EXPERTS].set(b_router)

    past = cache_k_a.shape[2]
    ck_a = _dup_halves(cache_k_a.reshape(nbs, -1, past, A_KV_HEADS * HEAD_DIM), A_KV_HEADS).astype(BF16)
    cv_a = _dup_halves(cache_v_a.reshape(nbs, -1, past, A_KV_HEADS * HEAD_DIM), A_KV_HEADS).astype(BF16)
    ck_b = cache_k_b.reshape(nbs, -1, past, D_MODEL).astype(BF16)
    cv_b = cache_v_b.reshape(nbs, -1, past, D_MODEL).astype(BF16)

    wg_all = w_gate.astype(BF16)
    wu_all = w_up.astype(BF16)
    wd_all = w_down.astype(BF16)

    tok_spec = pl.BlockSpec((TM, D_MODEL), lambda i: (i, 0))
    state_specs = lambda w: pl.BlockSpec((TM, w), lambda i: (jnp.minimum(i, npt - 1), 0))

    new_k_a, new_v_a, new_k_b, new_v_b = [], [], [], []
    prev = None
    for l in range(depth):
        j = l // 2
        has_prev = prev is not None
        if has_prev:
            x1_prev, y_prev, l_prev = prev
            x_args = [x1_prev, y_prev, mod]
            x_specs = _x_specs(True, npt)
            x_specs[2] = _mod_spec(l_prev, 5, npt, tpb)
        else:
            x_args = [xp, xs]
            x_specs = _x_specs(False, npt)
        mod_args = lambda ks: [mod] * len(ks)
        mod_specs = lambda ks: [_mod_spec(l, k, npt, tpb) for k in ks]
        g_attn_l = g_attn[l].reshape(1, d)

        if l % 2 == 0:
            wq, wk, wv = jnp.split(w_qkv_a[j], [A_HEADS * HEAD_DIM, (A_HEADS + A_KV_HEADS) * HEAD_DIM], axis=-1)
            w_all = jnp.concatenate([wq, _dup_halves(wk, A_KV_HEADS), _dup_halves(wv, A_KV_HEADS)], axis=-1).astype(BF16)
            gvec = jnp.concatenate([jnp.tile(g_q_a[j], A_HEADS) * qscale,
                                    jnp.tile(g_k_a[j], 2 * A_KV_HEADS)]).reshape(1, nqk)
            wkv = 2 * A_KV_HEADS * HEAD_DIM
            q, k, v, ks, vs = pl.pallas_call(
                functools.partial(_proj_a_kernel, has_prev, npt),
                grid=(nti,),
                in_specs=x_specs + mod_specs((0, 1)) + [
                    _const_spec((1, d)), _const_spec(w_all.shape), _const_spec(eq.shape),
                    _const_spec(et.shape), _const_spec((1, nqk)), _rope_spec(npt, tpb), _rope_spec(npt, tpb)],
                out_specs=[tok_spec, pl.BlockSpec((TM, wkv), lambda i: (i, 0)),
                           pl.BlockSpec((TM, wkv), lambda i: (i, 0)),
                           state_specs(wkv // 2), state_specs(wkv // 2)],
                out_shape=[jax.ShapeDtypeStruct((t_all, d), BF16),
                           jax.ShapeDtypeStruct((t_all, wkv), BF16),
                           jax.ShapeDtypeStruct((t_all, wkv), BF16),
                           jax.ShapeDtypeStruct((tp, wkv // 2), F32),
                           jax.ShapeDtypeStruct((tp, wkv // 2), F32)],
                compiler_params=_cparams(1),
                name="proj_a",
            )(*x_args, *mod_args((0, 1)), g_attn_l, w_all, eq, et, gvec, cos_t, sin_t)
            new_k_a.append(ks.reshape(nbp, seq, A_KV_HEADS, HEAD_DIM))
            new_v_a.append(vs.reshape(nbp, seq, A_KV_HEADS, HEAD_DIM))
            o_p, o_s = _attention("a", q, k, v, nbp, seq, nbs, dseq, ck_a, cv_a, j)
            w_o = w_o_a[j].astype(BF16)
        else:
            lam_init = 0.8 - 0.6 * math.exp(-0.3 * l)
            lam = (jnp.exp(jnp.sum((lam_q1[j] * lam_k1[j]).astype(F32)))
                   - jnp.exp(jnp.sum((lam_q2[j] * lam_k2[j]).astype(F32))) + lam_init).reshape(1, 1)
            wq, wk, wv = jnp.split(w_qkv_b[j], 3, axis=-1)
            w_all = jnp.concatenate([wq * qscale, wk, wv], axis=-1).astype(BF16)
            q, k, v, ks, vs = pl.pallas_call(
                functools.partial(_proj_b_kernel, has_prev, npt),
                grid=(nti,),
                in_specs=x_specs + mod_specs((0, 1)) + [
                    _const_spec((1, d)), _const_spec(w_all.shape), _rope_spec(npt, tpb), _rope_spec(npt, tpb)],
                out_specs=[tok_spec, tok_spec, tok_spec, state_specs(d), state_specs(d)],
                out_shape=[jax.ShapeDtypeStruct((t_all, d), BF16),
                           jax.ShapeDtypeStruct((t_all, d), BF16),
                           jax.ShapeDtypeStruct((t_all, d), BF16),
                           jax.ShapeDtypeStruct((tp, d), F32),
                           jax.ShapeDtypeStruct((tp, d), F32)],
                compiler_params=_cparams(1),
                name="proj_b",
            )(*x_args, *mod_args((0, 1)), g_attn_l, w_all, cos_t, sin_t)
            new_k_b.append(ks.reshape(nbp, seq, B_HEADS, 2, HEAD_DIM))
            new_v_b.append(vs.reshape(nbp, seq, B_HEADS, 2 * HEAD_DIM))
            gsub = (g_sub_b[j] * (1.0 - lam_init)).reshape(1, 2 * HEAD_DIM)
            o_p, o_s = _attention("b", q, k, v, nbp, seq, nbs, dseq, ck_b, cv_b, j, lam=lam, gsub=gsub)
            w_o = w_o_b[j].astype(BF16)

        x1, h2, rt = pl.pallas_call(
            functools.partial(_kb_kernel, has_prev, npt),
            grid=(nti,),
            in_specs=x_specs + [
                pl.BlockSpec((TM, d), lambda i: (jnp.minimum(i, npt - 1), 0)),
                pl.BlockSpec((TM, d), lambda i: (jnp.maximum(i - npt, 0), 0))] + mod_specs((2, 3, 4)) + [
                _const_spec((d, d)), _const_spec((1, d)), _const_spec((d, LANES)), _const_spec((d, LANES)),
                _const_spec((1, LANES))],
            out_specs=[tok_spec, pl.BlockSpec((TM * N_CHUNKS, LANES), lambda i: (i, 0)),
                       pl.BlockSpec((8, TM), lambda i: (0, i))],
            out_shape=[jax.ShapeDtypeStruct((t_all, d), F32),
                       jax.ShapeDtypeStruct((t_all * N_CHUNKS, LANES), F32),
                       jax.ShapeDtypeStruct((8, t_all), F32)],
            scratch_shapes=[pltpu.VMEM((LANES, TM), F32)],
            compiler_params=_cparams(1),
            name="oproj_router",
        )(*x_args, o_p, o_s, *mod_args((2, 3, 4)), w_o, g_ffn[l].reshape(1, d), wr_hi, wr_lo, br)

        y = _moe(h2.reshape(t_all, N_CHUNKS, LANES), rt, wg_all[l], wu_all[l], wd_all[l], t_all)
        prev = (x1, y.reshape((t_all + R_MOE) * N_CHUNKS, LANES), l)

    x1_prev, y_prev, l_prev = prev
    y_p, y_s = pl.pallas_call(
        functools.partial(_final_kernel, npt),
        grid=(nti,),
        in_specs=[tok_spec, pl.BlockSpec((TM * N_CHUNKS, LANES), lambda i: (i, 0)),
                  _mod_spec(l_prev, 5, npt, tpb), _const_spec((1, d))],
        out_specs=[pl.BlockSpec((TM, d), lambda i: (jnp.minimum(i, npt - 1), 0)),
                   pl.BlockSpec((TM, d), lambda i: (jnp.maximum(i - npt, 0), 0))],
        out_shape=[jax.ShapeDtypeStruct((tp, d), F32), jax.ShapeDtypeStruct((ts, d), F32)],
        compiler_params=_cparams(1),
        name="final_norm",
    )(x1_prev, y_prev, mod, g_final.reshape(1, d))

    return (y_p.reshape(nbp, seq, d), y_s.reshape(nbs, dseq, d),
            jnp.stack(new_k_a, axis=1), jnp.stack(new_v_a, axis=1),
            jnp.stack(new_k_b, axis=1), jnp.stack(new_v_b, axis=1))
```

```python
import functools
import math

import jax
import jax.numpy as jnp
from jax import lax
from jax.experimental import pallas as pl
from jax.experimental.pallas import tpu as pltpu

F32 = jnp.float32
BF16 = jnp.bfloat16

D_MODEL = 1024
HEAD_DIM = 64
A_HEADS = 16
A_KV_HEADS = 4
B_HEADS = 8
N_EXPERTS = 16
N_GROUPS = 4
EXPERTS_PER_GROUP = 4
D_EXPERT = 512
N_MOD = 6
N_MOD_ROWS = 16
ROPE_THETA = 10000.0
GRID_W = 64
EPS = 1e-6

LANES = 128
MXU_COLS = 256
N_CHUNKS = D_MODEL // LANES
TM = 256
TQ_A = 128
TQ_B = 256
R_MOE = 256
N_PAIRS = 6
N_BUCKETS = N_GROUPS * N_PAIRS
PAIR_A = (0, 0, 0, 1, 1, 3)
PAIR_B = (1, 2, 3, 3, 2, 2)
VMEM_LIMIT = 56 * 1024 * 1024


def _cparams(n_axes=1):
    return pltpu.CompilerParams(dimension_semantics=("arbitrary",) * n_axes,
                                vmem_limit_bytes=VMEM_LIMIT)


def _slab_to_nat(ref, tm):
    return jnp.concatenate([ref[pl.ds(c, tm, stride=N_CHUNKS), :] for c in range(N_CHUNKS)], axis=-1)


def _nat_to_slab(ref, val, tm):
    for c in range(N_CHUNKS):
        ref[pl.ds(c, tm, stride=N_CHUNKS), :] = val[:, LANES * c:LANES * (c + 1)]


def _norm_mod(x, g, shift, scale):
    ms = jnp.mean(x * x, axis=-1, keepdims=True)
    return x * lax.rsqrt(ms + EPS) * g * (1.0 + scale) + shift


def _rope_cols(x, cos, sin, second_of_pair):
    outs = []
    for c in range(x.shape[1] // LANES):
        xc = x[:, LANES * c:LANES * (c + 1)]
        partner = jnp.where(second_of_pair, pltpu.roll(xc, 16, 1), pltpu.roll(xc, LANES - 16, 1))
        outs.append(xc * cos + partner * sin)
    return jnp.concatenate(outs, axis=-1)


def _stream_x(i, npt, refs, has_prev, tm):
    if has_prev:
        x1_ref, y_ref, g2_ref = refs
        return x1_ref[...] + g2_ref[0] * _slab_to_nat(y_ref, tm)
    xp_ref, xs_ref = refs
    return jnp.where(i < npt, xp_ref[...], xs_ref[...])


def _mod_kernel(cv_ref, w_ref, b_ref, o_ref):
    cv = cv_ref[...]
    s = cv * (1.0 / (1.0 + jnp.exp(-cv)))
    o_ref[0] = jnp.dot(s.astype(BF16), w_ref[0].astype(BF16), preferred_element_type=F32) + b_ref[0]


def _modulation(cvec, w_mod, b_mod):
    depth, d, n = w_mod.shape
    tn = 1024
    return pl.pallas_call(
        _mod_kernel,
        grid=(depth, n // tn),
        in_specs=[pl.BlockSpec((N_MOD_ROWS, d), lambda l, j: (0, 0)),
                  pl.BlockSpec((1, d, tn), lambda l, j: (l, 0, j)),
                  pl.BlockSpec((1, 1, tn), lambda l, j: (l, 0, j))],
        out_specs=pl.BlockSpec((1, N_MOD_ROWS, tn), lambda l, j: (l, 0, j)),
        out_shape=jax.ShapeDtypeStruct((depth, N_MOD_ROWS, n), F32),
        compiler_params=_cparams(2),
        name="modulation",
    )(cvec, w_mod, b_mod.reshape(depth, 1, n))


def _proj_a_kernel(has_prev, npt, *refs):
    n_x = 3 if has_prev else 2
    xrefs, refs = refs[:n_x], refs[n_x:]
    (sh_ref, sc_ref, ga_ref, w_ref, eq_ref, et_ref, gv_ref, cos_ref, sin_ref,
     q_ref, k_ref, v_ref, ks_ref, vs_ref) = refs
    i = pl.program_id(0)
    x = _stream_x(i, npt, xrefs, has_prev, TM)
    hb = _norm_mod(x, ga_ref[...], sh_ref[0], sc_ref[0]).astype(BF16)
    qkv = jnp.dot(hb, w_ref[...], preferred_element_type=F32)
    nqk = D_MODEL + 2 * A_KV_HEADS * HEAD_DIM
    qk = qkv[:, :nqk]
    ms = jnp.dot((qk * qk).astype(BF16), eq_ref[...], preferred_element_type=F32)
    inv = lax.rsqrt(ms + EPS)
    inv_hi = inv.astype(BF16)
    inv_lo = (inv - inv_hi.astype(F32)).astype(BF16)
    invf = jnp.dot(jnp.concatenate([inv_hi, inv_lo], axis=-1), et_ref[...], preferred_element_type=F32)
    qkn = qk * invf * gv_ref[...]
    lane = lax.broadcasted_iota(jnp.int32, (1, LANES), 1)
    second = (lane & 16) != 0
    lo = lane < HEAD_DIM
    qkr = _rope_cols(qkn, cos_ref[...], sin_ref[...], second)
    q_ref[...] = qkr[:, :D_MODEL].astype(BF16)
    k_ref[...] = qkr[:, D_MODEL:].astype(BF16)
    vd = qkv[:, nqk:]
    v_ref[...] = vd.astype(BF16)

    @pl.when(i < npt)
    def _():
        kd = qkn[:, D_MODEL:]
        for j in range(2):
            ks_ref[:, LANES * j:LANES * (j + 1)] = jnp.where(
                lo, kd[:, LANES * 2 * j:LANES * (2 * j + 1)], kd[:, LANES * (2 * j + 1):LANES * (2 * j + 2)])
            vs_ref[:, LANES * j:LANES * (j + 1)] = jnp.where(
                lo, vd[:, LANES * 2 * j:LANES * (2 * j + 1)], vd[:, LANES * (2 * j + 1):LANES * (2 * j + 2)])


def _proj_b_kernel(has_prev, npt, *refs):
    n_x = 3 if has_prev else 2
    xrefs, refs = refs[:n_x], refs[n_x:]
    (sh_ref, sc_ref, ga_ref, w_ref, cos_ref, sin_ref,
     q_ref, k_ref, v_ref, ks_ref, vs_ref) = refs
    i = pl.program_id(0)
    x = _stream_x(i, npt, xrefs, has_prev, TM)
    hb = _norm_mod(x, ga_ref[...], sh_ref[0], sc_ref[0]).astype(BF16)
    qkv = jnp.dot(hb, w_ref[...], preferred_element_type=F32)
    lane = lax.broadcasted_iota(jnp.int32, (1, LANES), 1)
    second = (lane & 16) != 0
    qk = qkv[:, :2 * D_MODEL]
    qkr = _rope_cols(qk, cos_ref[...], sin_ref[...], second)
    q_ref[...] = qkr[:, :D_MODEL].astype(BF16)
    k_ref[...] = qkr[:, D_MODEL:].astype(BF16)
    v = qkv[:, 2 * D_MODEL:]
    v_ref[...] = v.astype(BF16)

    @pl.when(i < npt)
    def _():
        ks_ref[...] = qk[:, D_MODEL:]
        vs_ref[...] = v


def _x_specs(has_prev, npt):
    if has_prev:
        return [pl.BlockSpec((TM, D_MODEL), lambda i: (i, 0)),
                pl.BlockSpec((TM * N_CHUNKS, LANES), lambda i: (i, 0)),
                None]
    return [pl.BlockSpec((TM, D_MODEL), lambda i: (jnp.minimum(i, npt - 1), 0)),
            pl.BlockSpec((TM, D_MODEL), lambda i: (jnp.maximum(i - npt, 0), 0))]


def _mod_spec(layer, k, npt, tpb):
    def index(i):
        row = jnp.where(i < npt, 0, 1 + (i - npt) // tpb)
        return ((layer * N_MOD_ROWS + row) * N_MOD + k, 0, 0)
    return pl.BlockSpec((1, 1, D_MODEL), index)


def _rope_spec(npt, tpb):
    return pl.BlockSpec((TM, LANES), lambda i: (jnp.where(i < npt, 0, 1 + (i - npt) % tpb), 0))


def _const_spec(shape):
    nd = len(shape)
    return pl.BlockSpec(shape, lambda i: (0,) * nd)


def _lane_masks():
    lane = lax.broadcasted_iota(jnp.int32, (1, LANES), 1)
    lo = lane < HEAD_DIM
    mlo = jnp.where(lo, 1.0, 0.0).astype(BF16)
    mhi = jnp.where(lo, 0.0, 1.0).astype(BF16)
    return lo, mlo, mhi


_NT_DIMS = (((1,), (1,)), ((), ()))


def _scores_stage(lhs, cols, lk, has_cache, k_ref, ck_ref, s_scr, m_scr):
    s = lax.dot_general(lhs, k_ref[:, cols], _NT_DIMS, preferred_element_type=F32)
    s_scr[:, 0:lk] = s
    mp = s[:, 0:LANES]
    for c in range(1, lk // LANES):
        mp = jnp.maximum(mp, s[:, LANES * c:LANES * (c + 1)])
    if has_cache:
        s2 = lax.dot_general(lhs, ck_ref[0, 0, :, cols], _NT_DIMS, preferred_element_type=F32)
        past = s2.shape[1]
        s_scr[:, lk:lk + past] = s2
        for c in range(past // LANES):
            mp = jnp.maximum(mp, s2[:, LANES * c:LANES * (c + 1)])
    m_scr[...] = mp


def _weighted_values(s_scr, m_scr, lk, has_cache, v_ref, cv_ref, cols):
    m = jnp.max(m_scr[...], axis=-1, keepdims=True)
    pb = jnp.exp2(s_scr[...] - m).astype(BF16)
    rhs = jnp.concatenate([v_ref[:, cols], jnp.ones((lk, LANES), BF16)], axis=-1)
    res = jnp.dot(pb[:, 0:lk], rhs, preferred_element_type=F32)
    if has_cache:
        cv = cv_ref[0, 0, :, cols]
        rhs2 = jnp.concatenate([cv, jnp.ones(cv.shape, BF16)], axis=-1)
        res = res + jnp.dot(pb[:, lk:], rhs2, preferred_element_type=F32)
    return res


def _attn_a_kernel(tq, has_cache, *refs):
    if has_cache:
        q_ref, k_ref, v_ref, ck_ref, cv_ref, o_ref, s0, s1, m0, m1 = refs
    else:
        q_ref, k_ref, v_ref, o_ref, s0, s1, m0, m1 = refs
        ck_ref = cv_ref = None
    lk = k_ref.shape[0]
    lo, mlo, mhi = _lane_masks()
    bufs = ((s0, m0), (s1, m1))

    def scores(g, s_scr, m_scr):
        qa = q_ref[:, LANES * 2 * g:LANES * (2 * g + 1)]
        qb = q_ref[:, LANES * (2 * g + 1):LANES * (2 * g + 2)]
        lhs = jnp.concatenate([qa * mlo, qa * mhi, qb * mlo, qb * mhi], axis=0)
        _scores_stage(lhs, slice(LANES * g, LANES * (g + 1)), lk, has_cache, k_ref, ck_ref, s_scr, m_scr)

    def finish(g, s_scr, m_scr):
        cols = slice(LANES * g, LANES * (g + 1))
        res = _weighted_values(s_scr, m_scr, lk, has_cache, v_ref, cv_ref, cols)
        o = res[:, 0:LANES] * (1.0 / res[:, LANES:])
        o_ref[:, LANES * 2 * g:LANES * (2 * g + 1)] = jnp.where(lo, o[0:tq], o[tq:2 * tq]).astype(BF16)
        o_ref[:, LANES * (2 * g + 1):LANES * (2 * g + 2)] = jnp.where(
            lo, o[2 * tq:3 * tq], o[3 * tq:4 * tq]).astype(BF16)

    scores(0, *bufs[0])
    for g in range(A_KV_HEADS):
        if g + 1 < A_KV_HEADS:
            scores(g + 1, *bufs[(g + 1) % 2])
        finish(g, *bufs[g % 2])


def _attn_b_kernel(tq, has_cache, *refs):
    if has_cache:
        lam_ref, q_ref, k_ref, v_ref, ck_ref, cv_ref, gs_ref, o_ref, s0, s1, m0, m1 = refs
    else:
        lam_ref, q_ref, k_ref, v_ref, gs_ref, o_ref, s0, s1, m0, m1 = refs
        ck_ref = cv_ref = None
    lk = k_ref.shape[0]
    lam = lam_ref[0, 0]
    _, mlo, mhi = _lane_masks()
    bufs = ((s0, m0), (s1, m1))

    def scores(h, s_scr, m_scr):
        cols = slice(LANES * h, LANES * (h + 1))
        qc = q_ref[:, cols]
        lhs = jnp.concatenate([qc * mlo, qc * mhi], axis=0)
        _scores_stage(lhs, cols, lk, has_cache, k_ref, ck_ref, s_scr, m_scr)

    def finish(h, s_scr, m_scr):
        cols = slice(LANES * h, LANES * (h + 1))
        res = _weighted_values(s_scr, m_scr, lk, has_cache, v_ref, cv_ref, cols)
        on = res[:, 0:LANES] * (1.0 / res[:, LANES:])
        o = on[0:tq] - lam * on[tq:2 * tq]
        ms = jnp.mean(o * o, axis=-1, keepdims=True)
        o_ref[:, cols] = (o * lax.rsqrt(ms + EPS) * gs_ref[...]).astype(BF16)

    scores(0, *bufs[0])
    for h in range(B_HEADS):
        if h + 1 < B_HEADS:
            scores(h + 1, *bufs[(h + 1) % 2])
        finish(h, *bufs[h % 2])


def _attention(kind, q, k, v, nbp, seq, nbs, dseq, cache_k, cache_v, layer_j, lam=None, gsub=None):
    t_all = q.shape[0]
    wk = k.shape[1]
    tq = TQ_A if kind == "a" else TQ_B
    kern = _attn_a_kernel if kind == "a" else _attn_b_kernel
    pre_args, pre_specs_1, pre_specs_2 = [], [], []
    post_args, post_specs_1, post_specs_2 = [], [], []
    if kind == "b":
        pre_args = [lam]
        pre_specs_1 = [pl.BlockSpec(memory_space=pltpu.SMEM)]
        pre_specs_2 = [pl.BlockSpec(memory_space=pltpu.SMEM)]
        post_args = [gsub]
        post_specs_1 = [pl.BlockSpec((1, LANES), lambda b: (0, 0))]
        post_specs_2 = [pl.BlockSpec((1, LANES), lambda b, j: (0, 0))]

    stack = 4 if kind == "a" else 2

    def scratch(rows, keys):
        return [pltpu.VMEM((rows, keys), F32), pltpu.VMEM((rows, keys), F32),
                pltpu.VMEM((rows, LANES), F32), pltpu.VMEM((rows, LANES), F32)]

    tqp = min(tq, seq)
    nqp = seq // tqp
    out_p = pl.pallas_call(
        functools.partial(kern, tqp, False),
        grid=(nbp * nqp,),
        in_specs=pre_specs_1 + [
            pl.BlockSpec((tqp, D_MODEL), lambda b: (b, 0)),
            pl.BlockSpec((seq, wk), lambda b: (b // nqp, 0)),
            pl.BlockSpec((seq, wk), lambda b: (b // nqp, 0))] + post_specs_1,
        out_specs=pl.BlockSpec((tqp, D_MODEL), lambda b: (b, 0)),
        out_shape=jax.ShapeDtypeStruct((nbp * seq, D_MODEL), BF16),
        scratch_shapes=scratch(stack * tqp, seq),
        compiler_params=_cparams(1),
        name="attn_%s_ctx" % kind,
    )(*pre_args, q, k, v, *post_args)

    nq = dseq // tq
    qoff = (nbp * seq) // tq
    koff = (nbp * seq) // dseq
    assert (nbp * seq) % dseq == 0
    past = cache_k.shape[2]
    out_s = pl.pallas_call(
        functools.partial(kern, tq, True),
        grid=(nbs, nq),
        in_specs=pre_specs_2 + [
            pl.BlockSpec((tq, D_MODEL), lambda b, j: (qoff + b * nq + j, 0)),
            pl.BlockSpec((dseq, wk), lambda b, j: (koff + b, 0)),
            pl.BlockSpec((dseq, wk), lambda b, j: (koff + b, 0)),
            pl.BlockSpec((1, 1, past, wk), lambda b, j: (b, layer_j, 0, 0)),
            pl.BlockSpec((1, 1, past, wk), lambda b, j: (b, layer_j, 0, 0))] + post_specs_2,
        out_specs=pl.BlockSpec((tq, D_MODEL), lambda b, j: (b * nq + j, 0)),
        out_shape=jax.ShapeDtypeStruct((nbs * dseq, D_MODEL), BF16),
        scratch_shapes=scratch(stack * tq, dseq + past),
        compiler_params=_cparams(2),
        name="attn_%s_lat" % kind,
    )(*pre_args, q, k, v, cache_k, cache_v, *post_args)
    del t_all
    return out_p, out_s


def _first_max_index(vals, m):
    idx = jnp.full_like(m, float(len(vals) - 1))
    for j in range(len(vals) - 2, -1, -1):
        idx = jnp.where(vals[j] == m, float(j), idx)
    return idx


def _select(idx, vals):
    out = vals[-1]
    for j in range(len(vals) - 2, -1, -1):
        out = jnp.where(idx == float(j), vals[j], out)
    return out


def _kb_kernel(has_prev, npt, *refs):
    n_x = 3 if has_prev else 2
    xrefs, refs = refs[:n_x], refs[n_x:]
    (op_ref, os_ref, g1_ref, sh_ref, sc_ref, wo_ref, gf_ref, wrh_ref, wrl_ref, br_ref,
     x1_ref, h2_ref, rt_ref, lt_scr) = refs
    i = pl.program_id(0)
    x = _stream_x(i, npt, xrefs, has_prev, TM)
    o = jnp.where(i < npt, op_ref[...], os_ref[...])
    a = jnp.dot(o, wo_ref[...], preferred_element_type=F32)
    x1 = x + g1_ref[0] * a
    x1_ref[...] = x1
    h2 = _norm_mod(x1, gf_ref[...], sh_ref[0], sc_ref[0])
    _nat_to_slab(h2_ref, h2, TM)
    h_hi = h2.astype(BF16)
    h_lo = (h2 - h_hi.astype(F32)).astype(BF16)
    logits = (jnp.dot(h_hi, wrh_ref[...], preferred_element_type=F32)
              + jnp.dot(h_hi, wrl_ref[...], preferred_element_type=F32)
              + jnp.dot(h_lo, wrh_ref[...], preferred_element_type=F32)
              + br_ref[...])
    lt_scr[...] = logits.T
    sc = []
    for e in range(N_EXPERTS):
        z = lt_scr[e:e + 1, :]
        sc.append(1.0 / (1.0 + jnp.exp(-z)))
    ninf = jnp.float32(-jnp.inf)
    gscore, gtop = [], []
    for g in range(N_GROUPS):
        a4 = sc[EXPERTS_PER_GROUP * g:EXPERTS_PER_GROUP * (g + 1)]
        m1 = jnp.maximum(jnp.maximum(a4[0], a4[1]), jnp.maximum(a4[2], a4[3]))
        i1 = _first_max_index(a4, m1)
        b4 = [jnp.where(i1 == float(j), ninf, a4[j]) for j in range(EXPERTS_PER_GROUP)]
        m2 = jnp.maximum(jnp.maximum(b4[0], b4[1]), jnp.maximum(b4[2], b4[3]))
        i2 = _first_max_index(b4, m2)
        gscore.append(m1 + m2)
        gtop.append((m1, i1, m2, i2))
    gm = jnp.maximum(jnp.maximum(gscore[0], gscore[1]), jnp.maximum(gscore[2], gscore[3]))
    best = _first_max_index(gscore, gm)
    m1 = _select(best, [t[0] for t in gtop])
    i1 = _select(best, [t[1] for t in gtop])
    m2 = _select(best, [t[2] for t in gtop])
    i2 = _select(best, [t[3] for t in gtop])
    den = m1 + m2
    first_low = i1 < i2
    ia = jnp.minimum(i1, i2)
    ib = jnp.maximum(i1, i2)
    pid = jnp.where(ia == 0.0, ib - 1.0, jnp.where(ia == 1.0, 6.0 - ib, 5.0))
    g_low = jnp.where(first_low, m1, m2) / den
    g_high = jnp.where(first_low, m2, m1) / den
    swapped = pid == 5.0
    rt_ref[...] = jnp.zeros(rt_ref.shape, F32)
    rt_ref[0:1, :] = best * float(N_PAIRS) + pid
    rt_ref[1:2, :] = jnp.where(swapped, g_high, g_low)
    rt_ref[2:3, :] = jnp.where(swapped, g_low, g_high)


def _moe_kernel(r_tile, ea_ref, eb_ref, nv_ref,
                gcur_ref, gnext_ref, sprev_ref, scur_ref, gates_ref, h2_hbm,
                wga_ref, wua_ref, wda_ref, wgb_ref, wub_ref, wdb_ref,
                y_hbm, xbuf, ybuf, gsem, ssem):
    t = pl.program_id(0)
    nvalid = nv_ref[0]
    slot = t % 2

    def gather_start(idx_ref, s, rows=None):
        for r in (range(r_tile) if rows is None else rows):
            pltpu.make_async_copy(h2_hbm.at[idx_ref[0, 0, r]],
                                  xbuf.at[s, pl.ds(N_CHUNKS * r, N_CHUNKS)], gsem.at[s]).start(priority=r % 2)

    def gather_wait(s):
        pltpu.make_async_copy(xbuf.at[s], xbuf.at[s], gsem.at[s]).wait()

    def scatter_start(idx_ref, s, rows=None):
        for r in (range(r_tile) if rows is None else rows):
            pltpu.make_async_copy(ybuf.at[s, pl.ds(N_CHUNKS * r, N_CHUNKS)],
                                  y_hbm.at[idx_ref[0, 0, r]], ssem).start(priority=r % 2)

    def scatter_wait():
        pltpu.make_async_copy(ybuf.at[0], ybuf.at[0], ssem).wait()

    @pl.when(t == 0)
    def _():
        ybuf[1] = jnp.zeros(ybuf.shape[1:], F32)
        gather_start(gcur_ref, 0)

    @pl.when(t < nvalid)
    def _():
        @pl.when(t > 0)
        def _():
            scatter_wait()

        gather_wait(slot)
        x = _slab_to_nat(xbuf.at[slot], r_tile).astype(BF16)
        gts = gates_ref[...]
        n_groups = 2 * (2 * (D_EXPERT // MXU_COLS) + D_MODEL // MXU_COLS)
        per_group = r_tile // n_groups
        groups = iter(range(n_groups))

        def issue_copies():
            k = next(groups)
            rows = range(per_group * k, per_group * (k + 1))
            gather_start(gnext_ref, 1 - slot, rows)
            scatter_start(sprev_ref, 1 - slot, rows)

        y_cols = [None] * (D_MODEL // MXU_COLS)
        for (wg, wu, wd, col) in ((wga_ref, wua_ref, wda_ref, 0), (wgb_ref, wub_ref, wdb_ref, 1)):
            h_cols = []
            for c in range(D_EXPERT // MXU_COLS):
                cs = slice(MXU_COLS * c, MXU_COLS * (c + 1))
                g = jnp.dot(x, wg[0, :, cs], preferred_element_type=F32)
                issue_copies()
                u = jnp.dot(x, wu[0, :, cs], preferred_element_type=F32)
                issue_copies()
                h_cols.append((g * (1.0 / (1.0 + jnp.exp(-g))) * u).astype(BF16))
            hmid = jnp.concatenate(h_cols, axis=-1)
            gate = gts[:, col:col + 1]
            for c in range(D_MODEL // MXU_COLS):
                cs = slice(MXU_COLS * c, MXU_COLS * (c + 1))
                ye = gate * jnp.dot(hmid, wd[0, :, cs], preferred_element_type=F32)
                issue_copies()
                y_cols[c] = ye if y_cols[c] is None else y_cols[c] + ye
        assert next(groups, None) is None and per_group * n_groups == r_tile
        _nat_to_slab(ybuf.at[slot], jnp.concatenate(y_cols, axis=-1), r_tile)

        @pl.when(t == nvalid - 1)
        def _():
            gather_wait(1 - slot)
            scatter_wait()
            scatter_start(scur_ref, slot)
            scatter_wait()


def _moe(h2_slab, rt, wg, wu, wd, n_tok):
    r = R_MOE
    nt = n_tok // r + N_BUCKETS
    bucket = rt[0].astype(jnp.int32)
    _, order, gate_a, gate_b = lax.sort(
        (bucket, jnp.arange(n_tok, dtype=jnp.int32), rt[1], rt[2]), num_keys=1, is_stable=True)
    buckets = jnp.arange(N_BUCKETS, dtype=jnp.int32)
    counts = jnp.sum((bucket[:, None] == buckets[None, :]).astype(jnp.int32), axis=0)
    tiles_b = (counts + r - 1) // r
    tile_end = jnp.cumsum(tiles_b)
    tile_start = tile_end - tiles_b
    nvalid = tile_end[-1]
    first = jnp.cumsum(counts) - counts
    tidx = jnp.arange(nt, dtype=jnp.int32)
    tix = jnp.minimum(tidx, nvalid - 1)
    tb = jnp.minimum(jnp.sum((tix[:, None] >= tile_end[None, :]).astype(jnp.int32), axis=1), N_BUCKETS - 1)
    sel = (tb[:, None] == buckets[None, :]).astype(jnp.int32)
    pick = lambda v: jnp.sum(sel * v[None, :], axis=1)
    k = tidx - pick(tile_start)
    nrows = jnp.where(tidx < nvalid, jnp.clip(pick(counts) - k * r, 0, r), 0)
    p0 = jnp.clip(pick(first) + k * r, 0, n_tok)
    window = lambda v: jax.vmap(lambda s: lax.dynamic_slice(v, (s,), (r,)))(p0)
    pad = lambda v: jnp.concatenate([v, jnp.zeros((r,), v.dtype)])
    rows = jnp.arange(r, dtype=jnp.int32)[None, :]
    valid = rows < nrows[:, None]
    gidx = jnp.where(valid, window(pad(order)), 0)
    spare = jnp.broadcast_to(n_tok + rows, (nt, r))
    sidx = jnp.concatenate([spare[:1], jnp.where(valid, gidx, spare)], axis=0)
    gates = jnp.stack([jnp.where(valid, window(pad(gate_a)), 0.0),
                       jnp.where(valid, window(pad(gate_b)), 0.0)], axis=-1).reshape(nt * r, 2)
    grp = tb // N_PAIRS
    ea = (grp * EXPERTS_PER_GROUP + jnp.asarray(PAIR_A, jnp.int32)[tb % N_PAIRS]).astype(jnp.int32)
    eb = (grp * EXPERTS_PER_GROUP + jnp.asarray(PAIR_B, jnp.int32)[tb % N_PAIRS]).astype(jnp.int32)
    nv = nvalid.reshape(1).astype(jnp.int32)
    gidx3 = gidx.reshape(nt, 1, r)
    sidx3 = sidx.reshape(nt + 1, 1, r)

    def smem(index):
        return pl.BlockSpec((1, 1, r), lambda t, ea, eb, nv: (index(t), 0, 0), memory_space=pltpu.SMEM)

    w_in_a = pl.BlockSpec((1, D_MODEL, D_EXPERT), lambda t, ea, eb, nv: (ea[t], 0, 0))
    w_out_a = pl.BlockSpec((1, D_EXPERT, D_MODEL), lambda t, ea, eb, nv: (ea[t], 0, 0))
    w_in_b = pl.BlockSpec((1, D_MODEL, D_EXPERT), lambda t, ea, eb, nv: (eb[t], 0, 0))
    w_out_b = pl.BlockSpec((1, D_EXPERT, D_MODEL), lambda t, ea, eb, nv: (eb[t], 0, 0))
    grid_spec = pltpu.PrefetchScalarGridSpec(
        num_scalar_prefetch=3,
        grid=(nt,),
        in_specs=[smem(lambda t: t), smem(lambda t: jnp.minimum(t + 1, nt - 1)),
                  smem(lambda t: t), smem(lambda t: t + 1),
                  pl.BlockSpec((r, 2), lambda t, ea, eb, nv: (t, 0)),
                  pl.BlockSpec(memory_space=pl.ANY),
                  w_in_a, w_in_a, w_out_a, w_in_b, w_in_b, w_out_b],
        out_specs=pl.BlockSpec(memory_space=pl.ANY),
        scratch_shapes=[pltpu.VMEM((2, r * N_CHUNKS, LANES), F32),
                        pltpu.VMEM((2, r * N_CHUNKS, LANES), F32),
                        pltpu.SemaphoreType.DMA((2,)),
                        pltpu.SemaphoreType.DMA],
    )
    y = pl.pallas_call(
        functools.partial(_moe_kernel, r),
        grid_spec=grid_spec,
        out_shape=jax.ShapeDtypeStruct((n_tok + r, N_CHUNKS, LANES), F32),
        compiler_params=_cparams(1),
        name="moe_experts",
    )(ea, eb, nv, gidx3, gidx3, sidx3, sidx3, gates, h2_slab, wg, wu, wd, wg, wu, wd)
    return y


def _final_kernel(npt, x1_ref, y_ref, g2_ref, gf_ref, op_ref, os_ref):
    i = pl.program_id(0)
    x = x1_ref[...] + g2_ref[0] * _slab_to_nat(y_ref, TM)
    ms = jnp.mean(x * x, axis=-1, keepdims=True)
    out = x * lax.rsqrt(ms + EPS) * gf_ref[...]

    @pl.when(i < npt)
    def _():
        op_ref[...] = out

    @pl.when(i >= npt)
    def _():
        os_ref[...] = out


def _rope_tables(dseq):
    n_freq = HEAD_DIM // 4
    inv = ROPE_THETA ** (-jnp.arange(0, 2 * n_freq, 2, dtype=F32) / (2 * n_freq))
    pos = jnp.arange(dseq)
    row = (pos // GRID_W).astype(F32)
    col = (pos % GRID_W).astype(F32)
    ang_r = row[:, None] * inv
    ang_c = col[:, None] * inv
    ang = jnp.concatenate([ang_r, ang_r, ang_c, ang_c], axis=-1)
    sign = jnp.tile(jnp.concatenate([-jnp.ones((n_freq,), F32), jnp.ones((n_freq,), F32)]), 2)
    cos = jnp.tile(jnp.cos(ang), (1, 2))
    sin = jnp.tile(jnp.sin(ang) * sign, (1, 2))
    cos = jnp.concatenate([jnp.ones((TM, LANES), F32), cos], axis=0)
    sin = jnp.concatenate([jnp.zeros((TM, LANES), F32), sin], axis=0)
    return cos, sin


def _dup_halves(w, n_heads):
    lead = w.shape[:-1]
    w = w.reshape(lead + (n_heads, 1, HEAD_DIM))
    w = jnp.broadcast_to(w, lead + (n_heads, 2, HEAD_DIM))
    return w.reshape(lead + (n_heads * 2 * HEAD_DIM,))


def kernel(x_prompt, x_sample, cache_k_a, cache_v_a, cache_k_b, cache_v_b, c, c_ctx, w_mod, b_mod, g_attn, g_ffn, w_qkv_a, w_o_a, g_q_a, g_k_a, w_qkv_b, w_o_b, lam_q1, lam_k1, lam_q2, lam_k2, g_sub_b, w_router, b_router, w_gate, w_up, w_down, g_final):
    nbp, seq, d = x_prompt.shape
    nbs, dseq, _ = x_sample.shape
    depth = w_mod.shape[0]
    assert d == D_MODEL and seq % TM == 0 and dseq % TM == 0
    tp, ts = nbp * seq, nbs * dseq
    t_all = tp + ts
    npt = tp // TM
    tpb = dseq // TM
    nti = t_all // TM
    assert nbs + 1 <= N_MOD_ROWS

    xp = x_prompt.reshape(tp, d)
    xs = x_sample.reshape(ts, d)

    cvec = jnp.zeros((N_MOD_ROWS, d), F32).at[0].set(c_ctx).at[1:1 + nbs].set(c)
    mod = _modulation(cvec, w_mod, b_mod).reshape(depth * N_MOD_ROWS * N_MOD, 1, d)

    cos_t, sin_t = _rope_tables(dseq)
    qscale = HEAD_DIM ** -0.5 * math.log2(math.e)

    nqk = D_MODEL + 2 * A_KV_HEADS * HEAD_DIM
    lane_id = jnp.arange(nqk)
    seg = jnp.where(lane_id < D_MODEL, lane_id // HEAD_DIM, A_HEADS + (lane_id - D_MODEL) // (2 * HEAD_DIM))
    seg_w = jnp.where(lane_id < D_MODEL, 1.0 / HEAD_DIM, 1.0 / (2 * HEAD_DIM))
    onehot = (seg[:, None] == jnp.arange(LANES)[None, :]).astype(F32)
    eq = (onehot * seg_w[:, None]).astype(BF16)
    et = jnp.concatenate([onehot.T, onehot.T], axis=0).astype(BF16)

    w_router_p = jnp.zeros((d, LANES), F32).at[:, :N_EXPERTS].set(w_router)
    wr_hi = w_router_p.astype(BF16)
    wr_lo = (w_router_p - wr_hi.astype(F32)).astype(BF16)
    br = jnp.zeros((1, LANES), F32).at[0, :N_EXPERTS].set(b_router)

    past = cache_k_a.shape[2]
    ck_a = _dup_halves(cache_k_a.reshape(nbs, -1, past, A_KV_HEADS * HEAD_DIM), A_KV_HEADS).astype(BF16)
    cv_a = _dup_halves(cache_v_a.reshape(nbs, -1, past, A_KV_HEADS * HEAD_DIM), A_KV_HEADS).astype(BF16)
    ck_b = cache_k_b.reshape(nbs, -1, past, D_MODEL).astype(BF16)
    cv_b = cache_v_b.reshape(nbs, -1, past, D_MODEL).astype(BF16)

    wg_all = w_gate.astype(BF16)
    wu_all = w_up.astype(BF16)
    wd_all = w_down.astype(BF16)

    tok_spec = pl.BlockSpec((TM, D_MODEL), lambda i: (i, 0))
    state_specs = lambda w: pl.BlockSpec((TM, w), lambda i: (jnp.minimum(i, npt - 1), 0))

    new_k_a, new_v_a, new_k_b, new_v_b = [], [], [], []
    prev = None
    for l in range(depth):
        j = l // 2
        has_prev = prev is not None
        if has_prev:
            x1_prev, y_prev, l_prev = prev
            x_args = [x1_prev, y_prev, mod]
            x_specs = _x_specs(True, npt)
            x_specs[2] = _mod_spec(l_prev, 5, npt, tpb)
        else:
            x_args = [xp, xs]
            x_specs = _x_specs(False, npt)
        mod_args = lambda ks: [mod] * len(ks)
        mod_specs = lambda ks: [_mod_spec(l, k, npt, tpb) for k in ks]
        g_attn_l = g_attn[l].reshape(1, d)

        if l % 2 == 0:
            wq, wk, wv = jnp.split(w_qkv_a[j], [A_HEADS * HEAD_DIM, (A_HEADS + A_KV_HEADS) * HEAD_DIM], axis=-1)
            w_all = jnp.concatenate([wq, _dup_halves(wk, A_KV_HEADS), _dup_halves(wv, A_KV_HEADS)], axis=-1).astype(BF16)
            gvec = jnp.concatenate([jnp.tile(g_q_a[j], A_HEADS) * qscale,
                                    jnp.tile(g_k_a[j], 2 * A_KV_HEADS)]).reshape(1, nqk)
            wkv = 2 * A_KV_HEADS * HEAD_DIM
            q, k, v, ks, vs = pl.pallas_call(
                functools.partial(_proj_a_kernel, has_prev, npt),
                grid=(nti,),
                in_specs=x_specs + mod_specs((0, 1)) + [
                    _const_spec((1, d)), _const_spec(w_all.shape), _const_spec(eq.shape),
                    _const_spec(et.shape), _const_spec((1, nqk)), _rope_spec(npt, tpb), _rope_spec(npt, tpb)],
                out_specs=[tok_spec, pl.BlockSpec((TM, wkv), lambda i: (i, 0)),
                           pl.BlockSpec((TM, wkv), lambda i: (i, 0)),
                           state_specs(wkv // 2), state_specs(wkv // 2)],
                out_shape=[jax.ShapeDtypeStruct((t_all, d), BF16),
                           jax.ShapeDtypeStruct((t_all, wkv), BF16),
                           jax.ShapeDtypeStruct((t_all, wkv), BF16),
                           jax.ShapeDtypeStruct((tp, wkv // 2), F32),
                           jax.ShapeDtypeStruct((tp, wkv // 2), F32)],
                compiler_params=_cparams(1),
                name="proj_a",
            )(*x_args, *mod_args((0, 1)), g_attn_l, w_all, eq, et, gvec, cos_t, sin_t)
            new_k_a.append(ks.reshape(nbp, seq, A_KV_HEADS, HEAD_DIM))
            new_v_a.append(vs.reshape(nbp, seq, A_KV_HEADS, HEAD_DIM))
            o_p, o_s = _attention("a", q, k, v, nbp, seq, nbs, dseq, ck_a, cv_a, j)
            w_o = w_o_a[j].astype(BF16)
        else:
            lam_init = 0.8 - 0.6 * math.exp(-0.3 * l)
            lam = (jnp.exp(jnp.sum((lam_q1[j] * lam_k1[j]).astype(F32)))
                   - jnp.exp(jnp.sum((lam_q2[j] * lam_k2[j]).astype(F32))) + lam_init).reshape(1, 1)
            wq, wk, wv = jnp.split(w_qkv_b[j], 3, axis=-1)
            w_all = jnp.concatenate([wq * qscale, wk, wv], axis=-1).astype(BF16)
            q, k, v, ks, vs = pl.pallas_call(
                functools.partial(_proj_b_kernel, has_prev, npt),
                grid=(nti,),
                in_specs=x_specs + mod_specs((0, 1)) + [
                    _const_spec((1, d)), _const_spec(w_all.shape), _rope_spec(npt, tpb), _rope_spec(npt, tpb)],
                out_specs=[tok_spec, tok_spec, tok_spec, state_specs(d), state_specs(d)],
                out_shape=[jax.ShapeDtypeStruct((t_all, d), BF16),
                           jax.ShapeDtypeStruct((t_all, d), BF16),
                           jax.ShapeDtypeStruct((t_all, d), BF16),
                           jax.ShapeDtypeStruct((tp, d), F32),
                           jax.ShapeDtypeStruct((tp, d), F32)],
                compiler_params=_cparams(1),
                name="proj_b",
            )(*x_args, *mod_args((0, 1)), g_attn_l, w_all, cos_t, sin_t)
            new_k_b.append(ks.reshape(nbp, seq, B_HEADS, 2, HEAD_DIM))
            new_v_b.append(vs.reshape(nbp, seq, B_HEADS, 2 * HEAD_DIM))
            gsub = (g_sub_b[j] * (1.0 - lam_init)).reshape(1, 2 * HEAD_DIM)
            o_p, o_s = _attention("b", q, k, v, nbp, seq, nbs, dseq, ck_b, cv_b, j, lam=lam, gsub=gsub)
            w_o = w_o_b[j].astype(BF16)

        x1, h2, rt = pl.pallas_call(
            functools.partial(_kb_kernel, has_prev, npt),
            grid=(nti,),
            in_specs=x_specs + [
                pl.BlockSpec((TM, d), lambda i: (jnp.minimum(i, npt - 1), 0)),
                pl.BlockSpec((TM, d), lambda i: (jnp.maximum(i - npt, 0), 0))] + mod_specs((2, 3, 4)) + [
                _const_spec((d, d)), _const_spec((1, d)), _const_spec((d, LANES)), _const_spec((d, LANES)),
                _const_spec((1, LANES))],
            out_specs=[tok_spec, pl.BlockSpec((TM * N_CHUNKS, LANES), lambda i: (i, 0)),
                       pl.BlockSpec((8, TM), lambda i: (0, i))],
            out_shape=[jax.ShapeDtypeStruct((t_all, d), F32),
                       jax.ShapeDtypeStruct((t_all * N_CHUNKS, LANES), F32),
                       jax.ShapeDtypeStruct((8, t_all), F32)],
            scratch_shapes=[pltpu.VMEM((LANES, TM), F32)],
            compiler_params=_cparams(1),
            name="oproj_router",
        )(*x_args, o_p, o_s, *mod_args((2, 3, 4)), w_o, g_ffn[l].reshape(1, d), wr_hi, wr_lo, br)

        y = _moe(h2.reshape(t_all, N_CHUNKS, LANES), rt, wg_all[l], wu_all[l], wd_all[l], t_all)
        prev = (x1, y.reshape((t_all + R_MOE) * N_CHUNKS, LANES), l)

    x1_prev, y_prev, l_prev = prev
    y_p, y_s = pl.pallas_call(
        functools.partial(_final_kernel, npt),
        grid=(nti,),
        in_specs=[tok_spec, pl.BlockSpec((TM * N_CHUNKS, LANES), lambda i: (i, 0)),
                  _mod_spec(l_prev, 5, npt, tpb), _const_spec((1, d))],
        out_specs=[pl.BlockSpec((TM, d), lambda i: (jnp.minimum(i, npt - 1), 0)),
                   pl.BlockSpec((TM, d), lambda i: (jnp.maximum(i - npt, 0), 0))],
        out_shape=[jax.ShapeDtypeStruct((tp, d), F32), jax.ShapeDtypeStruct((ts, d), F32)],
        compiler_params=_cparams(1),
        name="final_norm",
    )(x1_prev, y_prev, mod, g_final.reshape(1, d))

    return (y_p.reshape(nbp, seq, d), y_s.reshape(nbs, dseq, d),
            jnp.stack(new_k_a, axis=1), jnp.stack(new_v_a, axis=1),
            jnp.stack(new_k_b, axis=1), jnp.stack(new_v_b, axis=1))
```

```python
import functools
import math

import jax
import jax.numpy as jnp
from jax import lax
from jax.experimental import pallas as pl
from jax.experimental.pallas import tpu as pltpu

F32 = jnp.float32
BF16 = jnp.bfloat16

D_MODEL = 1024
HEAD_DIM = 64
A_HEADS = 16
A_KV_HEADS = 4
B_HEADS = 8
N_EXPERTS = 16
N_GROUPS = 4
EXPERTS_PER_GROUP = 4
D_EXPERT = 512
N_MOD = 6
N_MOD_ROWS = 16
ROPE_THETA = 10000.0
GRID_W = 64
EPS = 1e-6

LANES = 128
MXU_COLS = 256
N_CHUNKS = D_MODEL // LANES
TM = 256
TQ_A = 128
TQ_B = 256
R_MOE = 256
N_PAIRS = 6
N_BUCKETS = N_GROUPS * N_PAIRS
PAIR_A = (0, 0, 0, 1, 1, 3)
PAIR_B = (1, 2, 3, 3, 2, 2)
MOE_COPY_PRIORITIES = ((0, 0), (1, 0), (1, "alternate"), ("alternate", "alternate"))
VMEM_LIMIT = 56 * 1024 * 1024


def _cparams(n_axes=1):
    return pltpu.CompilerParams(dimension_semantics=("arbitrary",) * n_axes,
                                vmem_limit_bytes=VMEM_LIMIT)


def _slab_to_nat(ref, tm):
    return jnp.concatenate([ref[pl.ds(c, tm, stride=N_CHUNKS), :] for c in range(N_CHUNKS)], axis=-1)


def _nat_to_slab(ref, val, tm):
    for c in range(N_CHUNKS):
        ref[pl.ds(c, tm, stride=N_CHUNKS), :] = val[:, LANES * c:LANES * (c + 1)]


def _norm_mod(x, g, shift, scale):
    ms = jnp.mean(x * x, axis=-1, keepdims=True)
    return x * lax.rsqrt(ms + EPS) * g * (1.0 + scale) + shift


def _rope_cols(x, cos, sin, second_of_pair):
    outs = []
    for c in range(x.shape[1] // LANES):
        xc = x[:, LANES * c:LANES * (c + 1)]
        partner = jnp.where(second_of_pair, pltpu.roll(xc, 16, 1), pltpu.roll(xc, LANES - 16, 1))
        outs.append(xc * cos + partner * sin)
    return jnp.concatenate(outs, axis=-1)


def _stream_x(i, npt, refs, has_prev, tm):
    if has_prev:
        x1_ref, y_ref, g2_ref = refs
        return x1_ref[...] + g2_ref[0] * _slab_to_nat(y_ref, tm)
    xp_ref, xs_ref = refs
    return jnp.where(i < npt, xp_ref[...], xs_ref[...])


def _mod_kernel(cv_ref, w_ref, b_ref, o_ref):
    cv = cv_ref[...]
    s = cv * (1.0 / (1.0 + jnp.exp(-cv)))
    o_ref[0] = jnp.dot(s.astype(BF16), w_ref[0].astype(BF16), preferred_element_type=F32) + b_ref[0]


def _modulation(cvec, w_mod, b_mod):
    depth, d, n = w_mod.shape
    tn = 1024
    return pl.pallas_call(
        _mod_kernel,
        grid=(depth, n // tn),
        in_specs=[pl.BlockSpec((N_MOD_ROWS, d), lambda l, j: (0, 0)),
                  pl.BlockSpec((1, d, tn), lambda l, j: (l, 0, j)),
                  pl.BlockSpec((1, 1, tn), lambda l, j: (l, 0, j))],
        out_specs=pl.BlockSpec((1, N_MOD_ROWS, tn), lambda l, j: (l, 0, j)),
        out_shape=jax.ShapeDtypeStruct((depth, N_MOD_ROWS, n), F32),
        compiler_params=_cparams(2),
        name="modulation",
    )(cvec, w_mod, b_mod.reshape(depth, 1, n))


def _proj_a_kernel(has_prev, npt, *refs):
    n_x = 3 if has_prev else 2
    xrefs, refs = refs[:n_x], refs[n_x:]
    (sh_ref, sc_ref, ga_ref, w_ref, eq_ref, et_ref, gv_ref, cos_ref, sin_ref,
     q_ref, k_ref, v_ref, ks_ref, vs_ref) = refs
    i = pl.program_id(0)
    x = _stream_x(i, npt, xrefs, has_prev, TM)
    hb = _norm_mod(x, ga_ref[...], sh_ref[0], sc_ref[0]).astype(BF16)
    qkv = jnp.dot(hb, w_ref[...], preferred_element_type=F32)
    nqk = D_MODEL + 2 * A_KV_HEADS * HEAD_DIM
    qk = qkv[:, :nqk]
    ms = jnp.dot((qk * qk).astype(BF16), eq_ref[...], preferred_element_type=F32)
    inv = lax.rsqrt(ms + EPS)
    inv_hi = inv.astype(BF16)
    inv_lo = (inv - inv_hi.astype(F32)).astype(BF16)
    invf = jnp.dot(jnp.concatenate([inv_hi, inv_lo], axis=-1), et_ref[...], preferred_element_type=F32)
    qkn = qk * invf * gv_ref[...]
    lane = lax.broadcasted_iota(jnp.int32, (1, LANES), 1)
    second = (lane & 16) != 0
    lo = lane < HEAD_DIM
    qkr = _rope_cols(qkn, cos_ref[...], sin_ref[...], second)
    q_ref[...] = qkr[:, :D_MODEL].astype(BF16)
    k_ref[...] = qkr[:, D_MODEL:].astype(BF16)
    vd = qkv[:, nqk:]
    v_ref[...] = vd.astype(BF16)

    @pl.when(i < npt)
    def _():
        kd = qkn[:, D_MODEL:]
        for j in range(2):
            ks_ref[:, LANES * j:LANES * (j + 1)] = jnp.where(
                lo, kd[:, LANES * 2 * j:LANES * (2 * j + 1)], kd[:, LANES * (2 * j + 1):LANES * (2 * j + 2)])
            vs_ref[:, LANES * j:LANES * (j + 1)] = jnp.where(
                lo, vd[:, LANES * 2 * j:LANES * (2 * j + 1)], vd[:, LANES * (2 * j + 1):LANES * (2 * j + 2)])


def _proj_b_kernel(has_prev, npt, *refs):
    n_x = 3 if has_prev else 2
    xrefs, refs = refs[:n_x], refs[n_x:]
    (sh_ref, sc_ref, ga_ref, w_ref, cos_ref, sin_ref,
     q_ref, k_ref, v_ref, ks_ref, vs_ref) = refs
    i = pl.program_id(0)
    x = _stream_x(i, npt, xrefs, has_prev, TM)
    hb = _norm_mod(x, ga_ref[...], sh_ref[0], sc_ref[0]).astype(BF16)
    qkv = jnp.dot(hb, w_ref[...], preferred_element_type=F32)
    lane = lax.broadcasted_iota(jnp.int32, (1, LANES), 1)
    second = (lane & 16) != 0
    qk = qkv[:, :2 * D_MODEL]
    qkr = _rope_cols(qk, cos_ref[...], sin_ref[...], second)
    q_ref[...] = qkr[:, :D_MODEL].astype(BF16)
    k_ref[...] = qkr[:, D_MODEL:].astype(BF16)
    v = qkv[:, 2 * D_MODEL:]
    v_ref[...] = v.astype(BF16)

    @pl.when(i < npt)
    def _():
        ks_ref[...] = qk[:, D_MODEL:]
        vs_ref[...] = v


def _x_specs(has_prev, npt):
    if has_prev:
        return [pl.BlockSpec((TM, D_MODEL), lambda i: (i, 0)),
                pl.BlockSpec((TM * N_CHUNKS, LANES), lambda i: (i, 0)),
                None]
    return [pl.BlockSpec((TM, D_MODEL), lambda i: (jnp.minimum(i, npt - 1), 0)),
            pl.BlockSpec((TM, D_MODEL), lambda i: (jnp.maximum(i - npt, 0), 0))]


def _mod_spec(layer, k, npt, tpb):
    def index(i):
        row = jnp.where(i < npt, 0, 1 + (i - npt) // tpb)
        return ((layer * N_MOD_ROWS + row) * N_MOD + k, 0, 0)
    return pl.BlockSpec((1, 1, D_MODEL), index)


def _rope_spec(npt, tpb):
    return pl.BlockSpec((TM, LANES), lambda i: (jnp.where(i < npt, 0, 1 + (i - npt) % tpb), 0))


def _const_spec(shape):
    nd = len(shape)
    return pl.BlockSpec(shape, lambda i: (0,) * nd)


def _lane_masks():
    lane = lax.broadcasted_iota(jnp.int32, (1, LANES), 1)
    lo = lane < HEAD_DIM
    mlo = jnp.where(lo, 1.0, 0.0).astype(BF16)
    mhi = jnp.where(lo, 0.0, 1.0).astype(BF16)
    return lo, mlo, mhi


_NT_DIMS = (((1,), (1,)), ((), ()))


def _scores_stage(lhs, cols, lk, has_cache, k_ref, ck_ref, s_scr, m_scr):
    s = lax.dot_general(lhs, k_ref[:, cols], _NT_DIMS, preferred_element_type=F32)
    s_scr[:, 0:lk] = s
    mp = s[:, 0:LANES]
    for c in range(1, lk // LANES):
        mp = jnp.maximum(mp, s[:, LANES * c:LANES * (c + 1)])
    if has_cache:
        s2 = lax.dot_general(lhs, ck_ref[0, 0, :, cols], _NT_DIMS, preferred_element_type=F32)
        past = s2.shape[1]
        s_scr[:, lk:lk + past] = s2
        for c in range(past // LANES):
            mp = jnp.maximum(mp, s2[:, LANES * c:LANES * (c + 1)])
    m_scr[...] = mp


def _weighted_values(s_scr, m_scr, lk, has_cache, v_ref, cv_ref, cols):
    m = jnp.max(m_scr[...], axis=-1, keepdims=True)
    pb = jnp.exp2(s_scr[...] - m).astype(BF16)
    rhs = jnp.concatenate([v_ref[:, cols], jnp.ones((lk, LANES), BF16)], axis=-1)
    res = jnp.dot(pb[:, 0:lk], rhs, preferred_element_type=F32)
    if has_cache:
        cv = cv_ref[0, 0, :, cols]
        rhs2 = jnp.concatenate([cv, jnp.ones(cv.shape, BF16)], axis=-1)
        res = res + jnp.dot(pb[:, lk:], rhs2, preferred_element_type=F32)
    return res


def _attn_a_kernel(tq, has_cache, *refs):
    if has_cache:
        q_ref, k_ref, v_ref, ck_ref, cv_ref, o_ref, s0, s1, m0, m1 = refs
    else:
        q_ref, k_ref, v_ref, o_ref, s0, s1, m0, m1 = refs
        ck_ref = cv_ref = None
    lk = k_ref.shape[0]
    lo, mlo, mhi = _lane_masks()
    bufs = ((s0, m0), (s1, m1))

    def scores(g, s_scr, m_scr):
        qa = q_ref[:, LANES * 2 * g:LANES * (2 * g + 1)]
        qb = q_ref[:, LANES * (2 * g + 1):LANES * (2 * g + 2)]
        lhs = jnp.concatenate([qa * mlo, qa * mhi, qb * mlo, qb * mhi], axis=0)
        _scores_stage(lhs, slice(LANES * g, LANES * (g + 1)), lk, has_cache, k_ref, ck_ref, s_scr, m_scr)

    def finish(g, s_scr, m_scr):
        cols = slice(LANES * g, LANES * (g + 1))
        res = _weighted_values(s_scr, m_scr, lk, has_cache, v_ref, cv_ref, cols)
        o = res[:, 0:LANES] * (1.0 / res[:, LANES:])
        o_ref[:, LANES * 2 * g:LANES * (2 * g + 1)] = jnp.where(lo, o[0:tq], o[tq:2 * tq]).astype(BF16)
        o_ref[:, LANES * (2 * g + 1):LANES * (2 * g + 2)] = jnp.where(
            lo, o[2 * tq:3 * tq], o[3 * tq:4 * tq]).astype(BF16)

    scores(0, *bufs[0])
    for g in range(A_KV_HEADS):
        if g + 1 < A_KV_HEADS:
            scores(g + 1, *bufs[(g + 1) % 2])
        finish(g, *bufs[g % 2])


def _attn_b_kernel(tq, has_cache, *refs):
    if has_cache:
        lam_ref, q_ref, k_ref, v_ref, ck_ref, cv_ref, gs_ref, o_ref, s0, s1, m0, m1 = refs
    else:
        lam_ref, q_ref, k_ref, v_ref, gs_ref, o_ref, s0, s1, m0, m1 = refs
        ck_ref = cv_ref = None
    lk = k_ref.shape[0]
    lam = lam_ref[0, 0]
    _, mlo, mhi = _lane_masks()
    bufs = ((s0, m0), (s1, m1))

    def scores(h, s_scr, m_scr):
        cols = slice(LANES * h, LANES * (h + 1))
        qc = q_ref[:, cols]
        lhs = jnp.concatenate([qc * mlo, qc * mhi], axis=0)
        _scores_stage(lhs, cols, lk, has_cache, k_ref, ck_ref, s_scr, m_scr)

    def finish(h, s_scr, m_scr):
        cols = slice(LANES * h, LANES * (h + 1))
        res = _weighted_values(s_scr, m_scr, lk, has_cache, v_ref, cv_ref, cols)
        on = res[:, 0:LANES] * (1.0 / res[:, LANES:])
        o = on[0:tq] - lam * on[tq:2 * tq]
        ms = jnp.mean(o * o, axis=-1, keepdims=True)
        o_ref[:, cols] = (o * lax.rsqrt(ms + EPS) * gs_ref[...]).astype(BF16)

    scores(0, *bufs[0])
    for h in range(B_HEADS):
        if h + 1 < B_HEADS:
            scores(h + 1, *bufs[(h + 1) % 2])
        finish(h, *bufs[h % 2])


def _attention(kind, q, k, v, nbp, seq, nbs, dseq, cache_k, cache_v, layer_j, lam=None, gsub=None):
    t_all = q.shape[0]
    wk = k.shape[1]
    tq = TQ_A if kind == "a" else TQ_B
    kern = _attn_a_kernel if kind == "a" else _attn_b_kernel
    pre_args, pre_specs_1, pre_specs_2 = [], [], []
    post_args, post_specs_1, post_specs_2 = [], [], []
    if kind == "b":
        pre_args = [lam]
        pre_specs_1 = [pl.BlockSpec(memory_space=pltpu.SMEM)]
        pre_specs_2 = [pl.BlockSpec(memory_space=pltpu.SMEM)]
        post_args = [gsub]
        post_specs_1 = [pl.BlockSpec((1, LANES), lambda b: (0, 0))]
        post_specs_2 = [pl.BlockSpec((1, LANES), lambda b, j: (0, 0))]

    stack = 4 if kind == "a" else 2

    def scratch(rows, keys):
        return [pltpu.VMEM((rows, keys), F32), pltpu.VMEM((rows, keys), F32),
                pltpu.VMEM((rows, LANES), F32), pltpu.VMEM((rows, LANES), F32)]

    tqp = min(tq, seq)
    nqp = seq // tqp
    out_p = pl.pallas_call(
        functools.partial(kern, tqp, False),
        grid=(nbp * nqp,),
        in_specs=pre_specs_1 + [
            pl.BlockSpec((tqp, D_MODEL), lambda b: (b, 0)),
            pl.BlockSpec((seq, wk), lambda b: (b // nqp, 0)),
            pl.BlockSpec((seq, wk), lambda b: (b // nqp, 0))] + post_specs_1,
        out_specs=pl.BlockSpec((tqp, D_MODEL), lambda b: (b, 0)),
        out_shape=jax.ShapeDtypeStruct((nbp * seq, D_MODEL), BF16),
        scratch_shapes=scratch(stack * tqp, seq),
        compiler_params=_cparams(1),
        name="attn_%s_ctx" % kind,
    )(*pre_args, q, k, v, *post_args)

    nq = dseq // tq
    qoff = (nbp * seq) // tq
    koff = (nbp * seq) // dseq
    assert (nbp * seq) % dseq == 0
    past = cache_k.shape[2]
    out_s = pl.pallas_call(
        functools.partial(kern, tq, True),
        grid=(nbs, nq),
        in_specs=pre_specs_2 + [
            pl.BlockSpec((tq, D_MODEL), lambda b, j: (qoff + b * nq + j, 0)),
            pl.BlockSpec((dseq, wk), lambda b, j: (koff + b, 0)),
            pl.BlockSpec((dseq, wk), lambda b, j: (koff + b, 0)),
            pl.BlockSpec((1, 1, past, wk), lambda b, j: (b, layer_j, 0, 0)),
            pl.BlockSpec((1, 1, past, wk), lambda b, j: (b, layer_j, 0, 0))] + post_specs_2,
        out_specs=pl.BlockSpec((tq, D_MODEL), lambda b, j: (b * nq + j, 0)),
        out_shape=jax.ShapeDtypeStruct((nbs * dseq, D_MODEL), BF16),
        scratch_shapes=scratch(stack * tq, dseq + past),
        compiler_params=_cparams(2),
        name="attn_%s_lat" % kind,
    )(*pre_args, q, k, v, cache_k, cache_v, *post_args)
    del t_all
    return out_p, out_s


def _first_max_index(vals, m):
    idx = jnp.full_like(m, float(len(vals) - 1))
    for j in range(len(vals) - 2, -1, -1):
        idx = jnp.where(vals[j] == m, float(j), idx)
    return idx


def _select(idx, vals):
    out = vals[-1]
    for j in range(len(vals) - 2, -1, -1):
        out = jnp.where(idx == float(j), vals[j], out)
    return out


def _kb_kernel(has_prev, npt, *refs):
    n_x = 3 if has_prev else 2
    xrefs, refs = refs[:n_x], refs[n_x:]
    (op_ref, os_ref, g1_ref, sh_ref, sc_ref, wo_ref, gf_ref, wrh_ref, wrl_ref, br_ref,
     x1_ref, h2_ref, rt_ref, lt_scr) = refs
    i = pl.program_id(0)
    x = _stream_x(i, npt, xrefs, has_prev, TM)
    o = jnp.where(i < npt, op_ref[...], os_ref[...])
    a = jnp.dot(o, wo_ref[...], preferred_element_type=F32)
    x1 = x + g1_ref[0] * a
    x1_ref[...] = x1
    h2 = _norm_mod(x1, gf_ref[...], sh_ref[0], sc_ref[0])
    _nat_to_slab(h2_ref, h2, TM)
    h_hi = h2.astype(BF16)
    h_lo = (h2 - h_hi.astype(F32)).astype(BF16)
    logits = (jnp.dot(h_hi, wrh_ref[...], preferred_element_type=F32)
              + jnp.dot(h_hi, wrl_ref[...], preferred_element_type=F32)
              + jnp.dot(h_lo, wrh_ref[...], preferred_element_type=F32)
              + br_ref[...])
    lt_scr[...] = logits.T
    sc = []
    for e in range(N_EXPERTS):
        z = lt_scr[e:e + 1, :]
        sc.append(1.0 / (1.0 + jnp.exp(-z)))
    ninf = jnp.float32(-jnp.inf)
    gscore, gtop = [], []
    for g in range(N_GROUPS):
        a4 = sc[EXPERTS_PER_GROUP * g:EXPERTS_PER_GROUP * (g + 1)]
        m1 = jnp.maximum(jnp.maximum(a4[0], a4[1]), jnp.maximum(a4[2], a4[3]))
        i1 = _first_max_index(a4, m1)
        b4 = [jnp.where(i1 == float(j), ninf, a4[j]) for j in range(EXPERTS_PER_GROUP)]
        m2 = jnp.maximum(jnp.maximum(b4[0], b4[1]), jnp.maximum(b4[2], b4[3]))
        i2 = _first_max_index(b4, m2)
        gscore.append(m1 + m2)
        gtop.append((m1, i1, m2, i2))
    gm = jnp.maximum(jnp.maximum(gscore[0], gscore[1]), jnp.maximum(gscore[2], gscore[3]))
    best = _first_max_index(gscore, gm)
    m1 = _select(best, [t[0] for t in gtop])
    i1 = _select(best, [t[1] for t in gtop])
    m2 = _select(best, [t[2] for t in gtop])
    i2 = _select(best, [t[3] for t in gtop])
    den = m1 + m2
    first_low = i1 < i2
    ia = jnp.minimum(i1, i2)
    ib = jnp.maximum(i1, i2)
    pid = jnp.where(ia == 0.0, ib - 1.0, jnp.where(ia == 1.0, 6.0 - ib, 5.0))
    g_low = jnp.where(first_low, m1, m2) / den
    g_high = jnp.where(first_low, m2, m1) / den
    swapped = pid == 5.0
    rt_ref[...] = jnp.zeros(rt_ref.shape, F32)
    rt_ref[0:1, :] = best * float(N_PAIRS) + pid
    rt_ref[1:2, :] = jnp.where(swapped, g_high, g_low)
    rt_ref[2:3, :] = jnp.where(swapped, g_low, g_high)


def _copy_priority(mode, r):
    return r % 2 if mode == "alternate" else mode


def _moe_kernel(r_tile, gather_prio, scatter_prio, ea_ref, eb_ref, nv_ref,
                gcur_ref, gnext_ref, sprev_ref, scur_ref, gates_ref, h2_hbm,
                wga_ref, wua_ref, wda_ref, wgb_ref, wub_ref, wdb_ref,
                y_hbm, xbuf, ybuf, gsem, ssem):
    t = pl.program_id(0)
    nvalid = nv_ref[0]
    slot = t % 2

    def gather_start(idx_ref, s, rows=None):
        for r in (range(r_tile) if rows is None else rows):
            pltpu.make_async_copy(h2_hbm.at[idx_ref[0, 0, r]],
                                  xbuf.at[s, pl.ds(N_CHUNKS * r, N_CHUNKS)],
                                  gsem.at[s]).start(priority=_copy_priority(gather_prio, r))

    def gather_wait(s):
        pltpu.make_async_copy(xbuf.at[s], xbuf.at[s], gsem.at[s]).wait()

    def scatter_start(idx_ref, s, rows=None):
        for r in (range(r_tile) if rows is None else rows):
            pltpu.make_async_copy(ybuf.at[s, pl.ds(N_CHUNKS * r, N_CHUNKS)],
                                  y_hbm.at[idx_ref[0, 0, r]],
                                  ssem).start(priority=_copy_priority(scatter_prio, r))

    def scatter_wait():
        pltpu.make_async_copy(ybuf.at[0], ybuf.at[0], ssem).wait()

    @pl.when(t == 0)
    def _():
        ybuf[1] = jnp.zeros(ybuf.shape[1:], F32)
        gather_start(gcur_ref, 0)

    @pl.when(t < nvalid)
    def _():
        @pl.when(t > 0)
        def _():
            scatter_wait()

        gather_wait(slot)
        x = _slab_to_nat(xbuf.at[slot], r_tile).astype(BF16)
        gts = gates_ref[...]
        n_groups = 2 * (2 * (D_EXPERT // MXU_COLS) + D_MODEL // MXU_COLS)
        per_group = r_tile // n_groups
        groups = iter(range(n_groups))

        def issue_copies():
            k = next(groups)
            rows = range(per_group * k, per_group * (k + 1))
            gather_start(gnext_ref, 1 - slot, rows)
            scatter_start(sprev_ref, 1 - slot, rows)

        y_cols = [None] * (D_MODEL // MXU_COLS)
        for (wg, wu, wd, col) in ((wga_ref, wua_ref, wda_ref, 0), (wgb_ref, wub_ref, wdb_ref, 1)):
            h_cols = []
            for c in range(D_EXPERT // MXU_COLS):
                cs = slice(MXU_COLS * c, MXU_COLS * (c + 1))
                g = jnp.dot(x, wg[0, :, cs], preferred_element_type=F32)
                issue_copies()
                u = jnp.dot(x, wu[0, :, cs], preferred_element_type=F32)
                issue_copies()
                h_cols.append((g * (1.0 / (1.0 + jnp.exp(-g))) * u).astype(BF16))
            hmid = jnp.concatenate(h_cols, axis=-1)
            gate = gts[:, col:col + 1]
            for c in range(D_MODEL // MXU_COLS):
                cs = slice(MXU_COLS * c, MXU_COLS * (c + 1))
                ye = gate * jnp.dot(hmid, wd[0, :, cs], preferred_element_type=F32)
                issue_copies()
                if col == 0:
                    y_cols[c] = ye
                else:
                    yc = y_cols[c] + ye
                    for half in range(MXU_COLS // LANES):
                        ybuf[slot, pl.ds(c * (MXU_COLS // LANES) + half, r_tile, stride=N_CHUNKS), :] = (
                            yc[:, LANES * half:LANES * (half + 1)])
        assert next(groups, None) is None and per_group * n_groups == r_tile

        @pl.when(t == nvalid - 1)
        def _():
            gather_wait(1 - slot)
            scatter_wait()
            scatter_start(scur_ref, slot)
            scatter_wait()


def _moe(h2_slab, rt, wg, wu, wd, n_tok, gather_prio, scatter_prio):
    r = R_MOE
    nt = n_tok // r + N_BUCKETS
    bucket = rt[0].astype(jnp.int32)
    n_pad = nt * r - n_tok
    buckets = jnp.arange(N_BUCKETS, dtype=jnp.int32)
    counts = jnp.sum((bucket[:, None] == buckets[None, :]).astype(jnp.int32), axis=0)
    tiles_b = (counts + r - 1) // r
    tile_end = jnp.cumsum(tiles_b)
    nvalid = tile_end[-1]
    pad_end = jnp.cumsum(tiles_b * r - counts)
    pad_bucket = jnp.sum((jnp.arange(n_pad, dtype=jnp.int32)[:, None] >= pad_end[None, :]).astype(jnp.int32),
                         axis=1)
    keys = jnp.concatenate([2 * bucket, 2 * pad_bucket + 1])
    zi = jnp.zeros((n_pad,), jnp.int32)
    zf = jnp.zeros((n_pad,), F32)
    _, gidx, valid, gate_a, gate_b = lax.sort(
        (keys, jnp.concatenate([jnp.arange(n_tok, dtype=jnp.int32), zi]),
         jnp.concatenate([jnp.ones((n_tok,), jnp.int32), zi]),
         jnp.concatenate([rt[1], zf]), jnp.concatenate([rt[2], zf])),
        num_keys=1, is_stable=True)
    rows = jnp.arange(r, dtype=jnp.int32)[None, :]
    spare = jnp.broadcast_to(n_tok + rows, (nt, r))
    sidx = jnp.concatenate([spare[:1], jnp.where(valid.reshape(nt, r) == 1, gidx.reshape(nt, r), spare)], axis=0)
    gates = jnp.stack([gate_a, gate_b], axis=-1)
    tix = jnp.minimum(jnp.arange(nt, dtype=jnp.int32), nvalid - 1)
    tb = jnp.minimum(jnp.sum((tix[:, None] >= tile_end[None, :]).astype(jnp.int32), axis=1), N_BUCKETS - 1)
    grp = tb // N_PAIRS
    ea = (grp * EXPERTS_PER_GROUP + jnp.asarray(PAIR_A, jnp.int32)[tb % N_PAIRS]).astype(jnp.int32)
    eb = (grp * EXPERTS_PER_GROUP + jnp.asarray(PAIR_B, jnp.int32)[tb % N_PAIRS]).astype(jnp.int32)
    nv = nvalid.reshape(1).astype(jnp.int32)
    gidx3 = gidx.reshape(nt, 1, r)
    sidx3 = sidx.reshape(nt + 1, 1, r)

    def smem(index):
        return pl.BlockSpec((1, 1, r), lambda t, ea, eb, nv: (index(t), 0, 0), memory_space=pltpu.SMEM)

    w_in_a = pl.BlockSpec((1, D_MODEL, D_EXPERT), lambda t, ea, eb, nv: (ea[t], 0, 0))
    w_out_a = pl.BlockSpec((1, D_EXPERT, D_MODEL), lambda t, ea, eb, nv: (ea[t], 0, 0))
    w_in_b = pl.BlockSpec((1, D_MODEL, D_EXPERT), lambda t, ea, eb, nv: (eb[t], 0, 0))
    w_out_b = pl.BlockSpec((1, D_EXPERT, D_MODEL), lambda t, ea, eb, nv: (eb[t], 0, 0))
    grid_spec = pltpu.PrefetchScalarGridSpec(
        num_scalar_prefetch=3,
        grid=(nt,),
        in_specs=[smem(lambda t: t), smem(lambda t: jnp.minimum(t + 1, nt - 1)),
                  smem(lambda t: t), smem(lambda t: t + 1),
                  pl.BlockSpec((r, 2), lambda t, ea, eb, nv: (t, 0)),
                  pl.BlockSpec(memory_space=pl.ANY),
                  w_in_a, w_in_a, w_out_a, w_in_b, w_in_b, w_out_b],
        out_specs=pl.BlockSpec(memory_space=pl.ANY),
        scratch_shapes=[pltpu.VMEM((2, r * N_CHUNKS, LANES), F32),
                        pltpu.VMEM((2, r * N_CHUNKS, LANES), F32),
                        pltpu.SemaphoreType.DMA((2,)),
                        pltpu.SemaphoreType.DMA],
    )
    y = pl.pallas_call(
        functools.partial(_moe_kernel, r, gather_prio, scatter_prio),
        grid_spec=grid_spec,
        out_shape=jax.ShapeDtypeStruct((n_tok + r, N_CHUNKS, LANES), F32),
        compiler_params=_cparams(1),
        name="moe_experts",
    )(ea, eb, nv, gidx3, gidx3, sidx3, sidx3, gates, h2_slab, wg, wu, wd, wg, wu, wd)
    return y


def _final_kernel(npt, x1_ref, y_ref, g2_ref, gf_ref, op_ref, os_ref):
    i = pl.program_id(0)
    x = x1_ref[...] + g2_ref[0] * _slab_to_nat(y_ref, TM)
    ms = jnp.mean(x * x, axis=-1, keepdims=True)
    out = x * lax.rsqrt(ms + EPS) * gf_ref[...]

    @pl.when(i < npt)
    def _():
        op_ref[...] = out

    @pl.when(i >= npt)
    def _():
        os_ref[...] = out


def _rope_tables(dseq):
    n_freq = HEAD_DIM // 4
    inv = ROPE_THETA ** (-jnp.arange(0, 2 * n_freq, 2, dtype=F32) / (2 * n_freq))
    pos = jnp.arange(dseq)
    row = (pos // GRID_W).astype(F32)
    col = (pos % GRID_W).astype(F32)
    ang_r = row[:, None] * inv
    ang_c = col[:, None] * inv
    ang = jnp.concatenate([ang_r, ang_r, ang_c, ang_c], axis=-1)
    sign = jnp.tile(jnp.concatenate([-jnp.ones((n_freq,), F32), jnp.ones((n_freq,), F32)]), 2)
    cos = jnp.tile(jnp.cos(ang), (1, 2))
    sin = jnp.tile(jnp.sin(ang) * sign, (1, 2))
    cos = jnp.concatenate([jnp.ones((TM, LANES), F32), cos], axis=0)
    sin = jnp.concatenate([jnp.zeros((TM, LANES), F32), sin], axis=0)
    return cos, sin


def _dup_halves(w, n_heads):
    lead = w.shape[:-1]
    w = w.reshape(lead + (n_heads, 1, HEAD_DIM))
    w = jnp.broadcast_to(w, lead + (n_heads, 2, HEAD_DIM))
    return w.reshape(lead + (n_heads * 2 * HEAD_DIM,))


def kernel(x_prompt, x_sample, cache_k_a, cache_v_a, cache_k_b, cache_v_b, c, c_ctx, w_mod, b_mod, g_attn, g_ffn, w_qkv_a, w_o_a, g_q_a, g_k_a, w_qkv_b, w_o_b, lam_q1, lam_k1, lam_q2, lam_k2, g_sub_b, w_router, b_router, w_gate, w_up, w_down, g_final):
    nbp, seq, d = x_prompt.shape
    nbs, dseq, _ = x_sample.shape
    depth = w_mod.shape[0]
    assert d == D_MODEL and seq % TM == 0 and dseq % TM == 0
    tp, ts = nbp * seq, nbs * dseq
    t_all = tp + ts
    npt = tp // TM
    tpb = dseq // TM
    nti = t_all // TM
    assert nbs + 1 <= N_MOD_ROWS

    xp = x_prompt.reshape(tp, d)
    xs = x_sample.reshape(ts, d)

    cvec = jnp.zeros((N_MOD_ROWS, d), F32).at[0].set(c_ctx).at[1:1 + nbs].set(c)
    mod = _modulation(cvec, w_mod, b_mod).reshape(depth * N_MOD_ROWS * N_MOD, 1, d)

    cos_t, sin_t = _rope_tables(dseq)
    qscale = HEAD_DIM ** -0.5 * math.log2(math.e)

    nqk = D_MODEL + 2 * A_KV_HEADS * HEAD_DIM
    lane_id = jnp.arange(nqk)
    seg = jnp.where(lane_id < D_MODEL, lane_id // HEAD_DIM, A_HEADS + (lane_id - D_MODEL) // (2 * HEAD_DIM))
    seg_w = jnp.where(lane_id < D_MODEL, 1.0 / HEAD_DIM, 1.0 / (2 * HEAD_DIM))
    onehot = (seg[:, None] == jnp.arange(LANES)[None, :]).astype(F32)
    eq = (onehot * seg_w[:, None]).astype(BF16)
    et = jnp.concatenate([onehot.T, onehot.T], axis=0).astype(BF16)

    w_router_p = jnp.zeros((d, LANES), F32).at[:, :N_EXPERTS].set(w_router)
    wr_hi = w_router_p.astype(BF16)
    wr_lo = (w_router_p - wr_hi.astype(F32)).astype(BF16)
    br = jnp.zeros((1, LANES), F32).at[0, :N_EXPERTS].set(b_router)

    past = cache_k_a.shape[2]
    ck_a = _dup_halves(cache_k_a.reshape(nbs, -1, past, A_KV_HEADS * HEAD_DIM), A_KV_HEADS).astype(BF16)
    cv_a = _dup_halves(cache_v_a.reshape(nbs, -1, past, A_KV_HEADS * HEAD_DIM), A_KV_HEADS).astype(BF16)
    ck_b = cache_k_b.reshape(nbs, -1, past, D_MODEL).astype(BF16)
    cv_b = cache_v_b.reshape(nbs, -1, past, D_MODEL).astype(BF16)

    wg_all = w_gate.astype(BF16)
    wu_all = w_up.astype(BF16)
    wd_all = w_down.astype(BF16)

    tok_spec = pl.BlockSpec((TM, D_MODEL), lambda i: (i, 0))
    state_specs = lambda w: pl.BlockSpec((TM, w), lambda i: (jnp.minimum(i, npt - 1), 0))

    new_k_a, new_v_a, new_k_b, new_v_b = [], [], [], []
    prev = None
    for l in range(depth):
        j = l // 2
        has_prev = prev is not None
        if has_prev:
            x1_prev, y_prev, l_prev = prev
            x_args = [x1_prev, y_prev, mod]
            x_specs = _x_specs(True, npt)
            x_specs[2] = _mod_spec(l_prev, 5, npt, tpb)
        else:
            x_args = [xp, xs]
            x_specs = _x_specs(False, npt)
        mod_args = lambda ks: [mod] * len(ks)
        mod_specs = lambda ks: [_mod_spec(l, k, npt, tpb) for k in ks]
        g_attn_l = g_attn[l].reshape(1, d)

        if l % 2 == 0:
            wq, wk, wv = jnp.split(w_qkv_a[j], [A_HEADS * HEAD_DIM, (A_HEADS + A_KV_HEADS) * HEAD_DIM], axis=-1)
            w_all = jnp.concatenate([wq, _dup_halves(wk, A_KV_HEADS), _dup_halves(wv, A_KV_HEADS)], axis=-1).astype(BF16)
            gvec = jnp.concatenate([jnp.tile(g_q_a[j], A_HEADS) * qscale,
                                    jnp.tile(g_k_a[j], 2 * A_KV_HEADS)]).reshape(1, nqk)
            wkv = 2 * A_KV_HEADS * HEAD_DIM
            q, k, v, ks, vs = pl.pallas_call(
                functools.partial(_proj_a_kernel, has_prev, npt),
                grid=(nti,),
                in_specs=x_specs + mod_specs((0, 1)) + [
                    _const_spec((1, d)), _const_spec(w_all.shape), _const_spec(eq.shape),
                    _const_spec(et.shape), _const_spec((1, nqk)), _rope_spec(npt, tpb), _rope_spec(npt, tpb)],
                out_specs=[tok_spec, pl.BlockSpec((TM, wkv), lambda i: (i, 0)),
                           pl.BlockSpec((TM, wkv), lambda i: (i, 0)),
                           state_specs(wkv // 2), state_specs(wkv // 2)],
                out_shape=[jax.ShapeDtypeStruct((t_all, d), BF16),
                           jax.ShapeDtypeStruct((t_all, wkv), BF16),
                           jax.ShapeDtypeStruct((t_all, wkv), BF16),
                           jax.ShapeDtypeStruct((tp, wkv // 2), F32),
                           jax.ShapeDtypeStruct((tp, wkv // 2), F32)],
                compiler_params=_cparams(1),
                name="proj_a",
            )(*x_args, *mod_args((0, 1)), g_attn_l, w_all, eq, et, gvec, cos_t, sin_t)
            new_k_a.append(ks.reshape(nbp, seq, A_KV_HEADS, HEAD_DIM))
            new_v_a.append(vs.reshape(nbp, seq, A_KV_HEADS, HEAD_DIM))
            o_p, o_s = _attention("a", q, k, v, nbp, seq, nbs, dseq, ck_a, cv_a, j)
            w_o = w_o_a[j].astype(BF16)
        else:
            lam_init = 0.8 - 0.6 * math.exp(-0.3 * l)
            lam = (jnp.exp(jnp.sum((lam_q1[j] * lam_k1[j]).astype(F32)))
                   - jnp.exp(jnp.sum((lam_q2[j] * lam_k2[j]).astype(F32))) + lam_init).reshape(1, 1)
            wq, wk, wv = jnp.split(w_qkv_b[j], 3, axis=-1)
            w_all = jnp.concatenate([wq * qscale, wk, wv], axis=-1).astype(BF16)
            q, k, v, ks, vs = pl.pallas_call(
                functools.partial(_proj_b_kernel, has_prev, npt),
                grid=(nti,),
                in_specs=x_specs + mod_specs((0, 1)) + [
                    _const_spec((1, d)), _const_spec(w_all.shape), _rope_spec(npt, tpb), _rope_spec(npt, tpb)],
                out_specs=[tok_spec, tok_spec, tok_spec, state_specs(d), state_specs(d)],
                out_shape=[jax.ShapeDtypeStruct((t_all, d), BF16),
                           jax.ShapeDtypeStruct((t_all, d), BF16),
                           jax.ShapeDtypeStruct((t_all, d), BF16),
                           jax.ShapeDtypeStruct((tp, d), F32),
                           jax.ShapeDtypeStruct((tp, d), F32)],
                compiler_params=_cparams(1),
                name="proj_b",
            )(*x_args, *mod_args((0, 1)), g_attn_l, w_all, cos_t, sin_t)
            new_k_b.append(ks.reshape(nbp, seq, B_HEADS, 2, HEAD_DIM))
            new_v_b.append(vs.reshape(nbp, seq, B_HEADS, 2 * HEAD_DIM))
            gsub = (g_sub_b[j] * (1.0 - lam_init)).reshape(1, 2 * HEAD_DIM)
            o_p, o_s = _attention("b", q, k, v, nbp, seq, nbs, dseq, ck_b, cv_b, j, lam=lam, gsub=gsub)
            w_o = w_o_b[j].astype(BF16)

        x1, h2, rt = pl.pallas_call(
            functools.partial(_kb_kernel, has_prev, npt),
            grid=(nti,),
            in_specs=x_specs + [
                pl.BlockSpec((TM, d), lambda i: (jnp.minimum(i, npt - 1), 0)),
                pl.BlockSpec((TM, d), lambda i: (jnp.maximum(i - npt, 0), 0))] + mod_specs((2, 3, 4)) + [
                _const_spec((d, d)), _const_spec((1, d)), _const_spec((d, LANES)), _const_spec((d, LANES)),
                _const_spec((1, LANES))],
            out_specs=[tok_spec, pl.BlockSpec((TM * N_CHUNKS, LANES), lambda i: (i, 0)),
                       pl.BlockSpec((8, TM), lambda i: (0, i))],
            out_shape=[jax.ShapeDtypeStruct((t_all, d), F32),
                       jax.ShapeDtypeStruct((t_all * N_CHUNKS, LANES), F32),
                       jax.ShapeDtypeStruct((8, t_all), F32)],
            scratch_shapes=[pltpu.VMEM((LANES, TM), F32)],
            compiler_params=_cparams(1),
            name="oproj_router",
        )(*x_args, o_p, o_s, *mod_args((2, 3, 4)), w_o, g_ffn[l].reshape(1, d), wr_hi, wr_lo, br)

        y = _moe(h2.reshape(t_all, N_CHUNKS, LANES), rt, wg_all[l], wu_all[l], wd_all[l], t_all,
                 *MOE_COPY_PRIORITIES[l % len(MOE_COPY_PRIORITIES)])
        prev = (x1, y.reshape((t_all + R_MOE) * N_CHUNKS, LANES), l)

    x1_prev, y_prev, l_prev = prev
    y_p, y_s = pl.pallas_call(
        functools.partial(_final_kernel, npt),
        grid=(nti,),
        in_specs=[tok_spec, pl.BlockSpec((TM * N_CHUNKS, LANES), lambda i: (i, 0)),
                  _mod_spec(l_prev, 5, npt, tpb), _const_spec((1, d))],
        out_specs=[pl.BlockSpec((TM, d), lambda i: (jnp.minimum(i, npt - 1), 0)),
                   pl.BlockSpec((TM, d), lambda i: (jnp.maximum(i - npt, 0), 0))],
        out_shape=[jax.ShapeDtypeStruct((tp, d), F32), jax.ShapeDtypeStruct((ts, d), F32)],
        compiler_params=_cparams(1),
        name="final_norm",
    )(x1_prev, y_prev, mod, g_final.reshape(1, d))

    return (y_p.reshape(nbp, seq, d), y_s.reshape(nbs, dseq, d),
            jnp.stack(new_k_a, axis=1), jnp.stack(new_v_a, axis=1),
            jnp.stack(new_k_b, axis=1), jnp.stack(new_v_b, axis=1))
```

```python
import functools
import math

import jax
import jax.numpy as jnp
from jax import lax
from jax.experimental import pallas as pl
from jax.experimental.pallas import tpu as pltpu

F32 = jnp.float32
BF16 = jnp.bfloat16

D_MODEL = 1024
HEAD_DIM = 64
A_HEADS = 16
A_KV_HEADS = 4
B_HEADS = 8
N_EXPERTS = 16
N_GROUPS = 4
EXPERTS_PER_GROUP = 4
D_EXPERT = 512
N_MOD = 6
N_MOD_ROWS = 16
ROPE_THETA = 10000.0
GRID_W = 64
EPS = 1e-6

LANES = 128
MXU_COLS = 256
N_CHUNKS = D_MODEL // LANES
N_PACKED = N_CHUNKS // 2
TM = 256
TQ_A = 128
TQ_B = 256
R_MOE = 256
N_PAIRS = 6
N_BUCKETS = N_GROUPS * N_PAIRS
PAIR_A = (0, 0, 0, 1, 1, 3)
PAIR_B = (1, 2, 3, 3, 2, 2)
VMEM_LIMIT = 56 * 1024 * 1024


def _cparams(n_axes=1):
    return pltpu.CompilerParams(dimension_semantics=("arbitrary",) * n_axes,
                                vmem_limit_bytes=VMEM_LIMIT)


def _slab_to_nat(ref, tm):
    return jnp.concatenate([ref[pl.ds(c, tm, stride=N_CHUNKS), :] for c in range(N_CHUNKS)], axis=-1)


def _nat_to_slab(ref, val, tm):
    for c in range(N_CHUNKS):
        ref[pl.ds(c, tm, stride=N_CHUNKS), :] = val[:, LANES * c:LANES * (c + 1)]


def _norm_mod(x, g, shift, scale):
    ms = jnp.mean(x * x, axis=-1, keepdims=True)
    return x * lax.rsqrt(ms + EPS) * g * (1.0 + scale) + shift


def _rope_cols(x, cos, sin, second_of_pair):
    outs = []
    for c in range(x.shape[1] // LANES):
        xc = x[:, LANES * c:LANES * (c + 1)]
        partner = jnp.where(second_of_pair, pltpu.roll(xc, 16, 1), pltpu.roll(xc, LANES - 16, 1))
        outs.append(xc * cos + partner * sin)
    return jnp.concatenate(outs, axis=-1)


def _stream_x(i, npt, refs, has_prev, tm):
    if has_prev:
        x1_ref, y_ref, g2_ref = refs
        return x1_ref[...] + g2_ref[0] * _slab_to_nat(y_ref, tm)
    xp_ref, xs_ref = refs
    return jnp.where(i < npt, xp_ref[...], xs_ref[...])


def _mod_kernel(cv_ref, w_ref, b_ref, o_ref):
    cv = cv_ref[...]
    s = cv * (1.0 / (1.0 + jnp.exp(-cv)))
    o_ref[0] = jnp.dot(s.astype(BF16), w_ref[0].astype(BF16), preferred_element_type=F32) + b_ref[0]


def _modulation(cvec, w_mod, b_mod):
    depth, d, n = w_mod.shape
    tn = 1024
    return pl.pallas_call(
        _mod_kernel,
        grid=(depth, n // tn),
        in_specs=[pl.BlockSpec((N_MOD_ROWS, d), lambda l, j: (0, 0)),
                  pl.BlockSpec((1, d, tn), lambda l, j: (l, 0, j)),
                  pl.BlockSpec((1, 1, tn), lambda l, j: (l, 0, j))],
        out_specs=pl.BlockSpec((1, N_MOD_ROWS, tn), lambda l, j: (l, 0, j)),
        out_shape=jax.ShapeDtypeStruct((depth, N_MOD_ROWS, n), F32),
        compiler_params=_cparams(2),
        name="modulation",
    )(cvec, w_mod, b_mod.reshape(depth, 1, n))


def _proj_a_kernel(has_prev, npt, *refs):
    n_x = 3 if has_prev else 2
    xrefs, refs = refs[:n_x], refs[n_x:]
    (sh_ref, sc_ref, ga_ref, w_ref, eq_ref, et_ref, gv_ref, cos_ref, sin_ref, _, _,
     q_ref, k_ref, v_ref, ks_ref, vs_ref) = refs
    i = pl.program_id(0)
    x = _stream_x(i, npt, xrefs, has_prev, TM)
    hb = _norm_mod(x, ga_ref[...], sh_ref[0], sc_ref[0]).astype(BF16)
    qkv = jnp.dot(hb, w_ref[...], preferred_element_type=F32)
    nqk = D_MODEL + 2 * A_KV_HEADS * HEAD_DIM
    qk = qkv[:, :nqk]
    ms = jnp.dot((qk * qk).astype(BF16), eq_ref[...], preferred_element_type=F32)
    inv = lax.rsqrt(ms + EPS)
    inv_hi = inv.astype(BF16)
    inv_lo = (inv - inv_hi.astype(F32)).astype(BF16)
    invf = jnp.dot(jnp.concatenate([inv_hi, inv_lo], axis=-1), et_ref[...], preferred_element_type=F32)
    qkn = qk * invf * gv_ref[...]
    lane = lax.broadcasted_iota(jnp.int32, (1, LANES), 1)
    second = (lane & 16) != 0
    lo = lane < HEAD_DIM
    qkr = _rope_cols(qkn, cos_ref[...], sin_ref[...], second)
    q_ref[...] = qkr[:, :D_MODEL].astype(BF16)
    k_ref[...] = qkr[:, D_MODEL:].astype(BF16)
    vd = qkv[:, nqk:]
    v_ref[...] = vd.astype(BF16)

    @pl.when(i < npt)
    def _():
        kd = qkn[:, D_MODEL:]
        for j in range(2):
            ks_ref[0, 0, :, LANES * j:LANES * (j + 1)] = jnp.where(
                lo, kd[:, LANES * 2 * j:LANES * (2 * j + 1)], kd[:, LANES * (2 * j + 1):LANES * (2 * j + 2)])
            vs_ref[0, 0, :, LANES * j:LANES * (j + 1)] = jnp.where(
                lo, vd[:, LANES * 2 * j:LANES * (2 * j + 1)], vd[:, LANES * (2 * j + 1):LANES * (2 * j + 2)])


def _proj_b_kernel(has_prev, npt, *refs):
    n_x = 3 if has_prev else 2
    xrefs, refs = refs[:n_x], refs[n_x:]
    (sh_ref, sc_ref, ga_ref, w_ref, cos_ref, sin_ref, _, _,
     q_ref, k_ref, v_ref, ks_ref, vs_ref) = refs
    i = pl.program_id(0)
    x = _stream_x(i, npt, xrefs, has_prev, TM)
    hb = _norm_mod(x, ga_ref[...], sh_ref[0], sc_ref[0]).astype(BF16)
    qkv = jnp.dot(hb, w_ref[...], preferred_element_type=F32)
    lane = lax.broadcasted_iota(jnp.int32, (1, LANES), 1)
    second = (lane & 16) != 0
    qk = qkv[:, :2 * D_MODEL]
    qkr = _rope_cols(qk, cos_ref[...], sin_ref[...], second)
    q_ref[...] = qkr[:, :D_MODEL].astype(BF16)
    k_ref[...] = qkr[:, D_MODEL:].astype(BF16)
    v = qkv[:, 2 * D_MODEL:]
    v_ref[...] = v.astype(BF16)

    @pl.when(i < npt)
    def _():
        ks_ref[0, 0] = qk[:, D_MODEL:]
        vs_ref[0, 0] = v


def _x_specs(has_prev, npt):
    if has_prev:
        return [pl.BlockSpec((TM, D_MODEL), lambda i: (i, 0)),
                pl.BlockSpec((TM * N_CHUNKS, LANES), lambda i: (i, 0)),
                None]
    return [pl.BlockSpec((TM, D_MODEL), lambda i: (jnp.minimum(i, npt - 1), 0)),
            pl.BlockSpec((TM, D_MODEL), lambda i: (jnp.maximum(i - npt, 0), 0))]


def _mod_spec(layer, k, npt, tpb):
    def index(i):
        row = jnp.where(i < npt, 0, 1 + (i - npt) // tpb)
        return ((layer * N_MOD_ROWS + row) * N_MOD + k, 0, 0)
    return pl.BlockSpec((1, 1, D_MODEL), index)


def _rope_spec(npt, tpb):
    return pl.BlockSpec((TM, LANES), lambda i: (jnp.where(i < npt, 0, 1 + (i - npt) % tpb), 0))


def _const_spec(shape):
    nd = len(shape)
    return pl.BlockSpec(shape, lambda i: (0,) * nd)


def _lane_masks():
    lane = lax.broadcasted_iota(jnp.int32, (1, LANES), 1)
    lo = lane < HEAD_DIM
    mlo = jnp.where(lo, 1.0, 0.0).astype(BF16)
    mhi = jnp.where(lo, 0.0, 1.0).astype(BF16)
    return lo, mlo, mhi


_NT_DIMS = (((1,), (1,)), ((), ()))


def _scores_stage(lhs, cols, lk, has_cache, k_ref, ck_ref, s_scr, m_scr):
    s = lax.dot_general(lhs, k_ref[:, cols], _NT_DIMS, preferred_element_type=F32)
    s_scr[:, 0:lk] = s
    mp = s[:, 0:LANES]
    for c in range(1, lk // LANES):
        mp = jnp.maximum(mp, s[:, LANES * c:LANES * (c + 1)])
    if has_cache:
        s2 = lax.dot_general(lhs, ck_ref[0, 0, :, cols], _NT_DIMS, preferred_element_type=F32)
        past = s2.shape[1]
        s_scr[:, lk:lk + past] = s2
        for c in range(past // LANES):
            mp = jnp.maximum(mp, s2[:, LANES * c:LANES * (c + 1)])
    m_scr[...] = mp


def _weighted_values(s_scr, m_scr, lk, has_cache, v_ref, cv_ref, cols):
    m = jnp.max(m_scr[...], axis=-1, keepdims=True)
    pb = jnp.exp2(s_scr[...] - m).astype(BF16)
    rhs = jnp.concatenate([v_ref[:, cols], jnp.ones((lk, LANES), BF16)], axis=-1)
    res = jnp.dot(pb[:, 0:lk], rhs, preferred_element_type=F32)
    if has_cache:
        cv = cv_ref[0, 0, :, cols]
        rhs2 = jnp.concatenate([cv, jnp.ones(cv.shape, BF16)], axis=-1)
        res = res + jnp.dot(pb[:, lk:], rhs2, preferred_element_type=F32)
    return res


def _attn_a_kernel(tq, has_cache, *refs):
    if has_cache:
        q_ref, k_ref, v_ref, ck_ref, cv_ref, o_ref, s0, s1, m0, m1 = refs
    else:
        q_ref, k_ref, v_ref, o_ref, s0, s1, m0, m1 = refs
        ck_ref = cv_ref = None
    lk = k_ref.shape[0]
    lo, mlo, mhi = _lane_masks()
    bufs = ((s0, m0), (s1, m1))

    def scores(g, s_scr, m_scr):
        qa = q_ref[:, LANES * 2 * g:LANES * (2 * g + 1)]
        qb = q_ref[:, LANES * (2 * g + 1):LANES * (2 * g + 2)]
        lhs = jnp.concatenate([qa * mlo, qa * mhi, qb * mlo, qb * mhi], axis=0)
        _scores_stage(lhs, slice(LANES * g, LANES * (g + 1)), lk, has_cache, k_ref, ck_ref, s_scr, m_scr)

    def finish(g, s_scr, m_scr):
        cols = slice(LANES * g, LANES * (g + 1))
        res = _weighted_values(s_scr, m_scr, lk, has_cache, v_ref, cv_ref, cols)
        o = res[:, 0:LANES] * (1.0 / res[:, LANES:])
        o_ref[:, LANES * 2 * g:LANES * (2 * g + 1)] = jnp.where(lo, o[0:tq], o[tq:2 * tq]).astype(BF16)
        o_ref[:, LANES * (2 * g + 1):LANES * (2 * g + 2)] = jnp.where(
            lo, o[2 * tq:3 * tq], o[3 * tq:4 * tq]).astype(BF16)

    scores(0, *bufs[0])
    for g in range(A_KV_HEADS):
        if g + 1 < A_KV_HEADS:
            scores(g + 1, *bufs[(g + 1) % 2])
        finish(g, *bufs[g % 2])


def _attn_b_kernel(tq, has_cache, *refs):
    if has_cache:
        lam_ref, q_ref, k_ref, v_ref, ck_ref, cv_ref, gs_ref, o_ref, s0, s1, m0, m1 = refs
    else:
        lam_ref, q_ref, k_ref, v_ref, gs_ref, o_ref, s0, s1, m0, m1 = refs
        ck_ref = cv_ref = None
    lk = k_ref.shape[0]
    lam = lam_ref[0, 0]
    _, mlo, mhi = _lane_masks()
    bufs = ((s0, m0), (s1, m1))

    def scores(h, s_scr, m_scr):
        cols = slice(LANES * h, LANES * (h + 1))
        qc = q_ref[:, cols]
        lhs = jnp.concatenate([qc * mlo, qc * mhi], axis=0)
        _scores_stage(lhs, cols, lk, has_cache, k_ref, ck_ref, s_scr, m_scr)

    def finish(h, s_scr, m_scr):
        cols = slice(LANES * h, LANES * (h + 1))
        res = _weighted_values(s_scr, m_scr, lk, has_cache, v_ref, cv_ref, cols)
        on = res[:, 0:LANES] * (1.0 / res[:, LANES:])
        o = on[0:tq] - lam * on[tq:2 * tq]
        ms = jnp.mean(o * o, axis=-1, keepdims=True)
        o_ref[:, cols] = (o * lax.rsqrt(ms + EPS) * gs_ref[...]).astype(BF16)

    scores(0, *bufs[0])
    for h in range(B_HEADS):
        if h + 1 < B_HEADS:
            scores(h + 1, *bufs[(h + 1) % 2])
        finish(h, *bufs[h % 2])


def _attention(kind, q, k, v, nbp, seq, nbs, dseq, cache_k, cache_v, layer_j, lam=None, gsub=None):
    t_all = q.shape[0]
    wk = k.shape[1]
    tq = TQ_A if kind == "a" else TQ_B
    kern = _attn_a_kernel if kind == "a" else _attn_b_kernel
    pre_args, pre_specs_1, pre_specs_2 = [], [], []
    post_args, post_specs_1, post_specs_2 = [], [], []
    if kind == "b":
        pre_args = [lam]
        pre_specs_1 = [pl.BlockSpec(memory_space=pltpu.SMEM)]
        pre_specs_2 = [pl.BlockSpec(memory_space=pltpu.SMEM)]
        post_args = [gsub]
        post_specs_1 = [pl.BlockSpec((1, LANES), lambda b: (0, 0))]
        post_specs_2 = [pl.BlockSpec((1, LANES), lambda b, j: (0, 0))]

    stack = 4 if kind == "a" else 2

    def scratch(rows, keys):
        return [pltpu.VMEM((rows, keys), F32), pltpu.VMEM((rows, keys), F32),
                pltpu.VMEM((rows, LANES), F32), pltpu.VMEM((rows, LANES), F32)]

    tqp = min(tq, seq)
    nqp = seq // tqp
    out_p = pl.pallas_call(
        functools.partial(kern, tqp, False),
        grid=(nbp * nqp,),
        in_specs=pre_specs_1 + [
            pl.BlockSpec((tqp, D_MODEL), lambda b: (b, 0)),
            pl.BlockSpec((seq, wk), lambda b: (b // nqp, 0)),
            pl.BlockSpec((seq, wk), lambda b: (b // nqp, 0))] + post_specs_1,
        out_specs=pl.BlockSpec((tqp, D_MODEL), lambda b: (b, 0)),
        out_shape=jax.ShapeDtypeStruct((nbp * seq, D_MODEL), BF16),
        scratch_shapes=scratch(stack * tqp, seq),
        compiler_params=_cparams(1),
        name="attn_%s_ctx" % kind,
    )(*pre_args, q, k, v, *post_args)

    nq = dseq // tq
    qoff = (nbp * seq) // tq
    koff = (nbp * seq) // dseq
    assert (nbp * seq) % dseq == 0
    past = cache_k.shape[2]
    out_s = pl.pallas_call(
        functools.partial(kern, tq, True),
        grid=(nbs, nq),
        in_specs=pre_specs_2 + [
            pl.BlockSpec((tq, D_MODEL), lambda b, j: (qoff + b * nq + j, 0)),
            pl.BlockSpec((dseq, wk), lambda b, j: (koff + b, 0)),
            pl.BlockSpec((dseq, wk), lambda b, j: (koff + b, 0)),
            pl.BlockSpec((1, 1, past, wk), lambda b, j: (b, layer_j, 0, 0)),
            pl.BlockSpec((1, 1, past, wk), lambda b, j: (b, layer_j, 0, 0))] + post_specs_2,
        out_specs=pl.BlockSpec((tq, D_MODEL), lambda b, j: (b * nq + j, 0)),
        out_shape=jax.ShapeDtypeStruct((nbs * dseq, D_MODEL), BF16),
        scratch_shapes=scratch(stack * tq, dseq + past),
        compiler_params=_cparams(2),
        name="attn_%s_lat" % kind,
    )(*pre_args, q, k, v, cache_k, cache_v, *post_args)
    del t_all
    return out_p, out_s


def _first_max_index(vals, m):
    idx = jnp.full_like(m, float(len(vals) - 1))
    for j in range(len(vals) - 2, -1, -1):
        idx = jnp.where(vals[j] == m, float(j), idx)
    return idx


def _select(idx, vals):
    out = vals[-1]
    for j in range(len(vals) - 2, -1, -1):
        out = jnp.where(idx == float(j), vals[j], out)
    return out


def _kb_kernel(has_prev, npt, *refs):
    n_x = 3 if has_prev else 2
    xrefs, refs = refs[:n_x], refs[n_x:]
    (op_ref, os_ref, g1_ref, sh_ref, sc_ref, wo_ref, gf_ref, wrh_ref, wrl_ref, br_ref,
     x1_ref, h2_ref, rt_ref, lt_scr) = refs
    i = pl.program_id(0)
    x = _stream_x(i, npt, xrefs, has_prev, TM)
    o = jnp.where(i < npt, op_ref[...], os_ref[...])
    a = jnp.dot(o, wo_ref[...], preferred_element_type=F32)
    x1 = x + g1_ref[0] * a
    x1_ref[...] = x1
    h2 = _norm_mod(x1, gf_ref[...], sh_ref[0], sc_ref[0])
    h_hi = h2.astype(BF16)
    bits = pltpu.bitcast(h_hi.astype(F32), jnp.uint32)
    words = bits[:, :D_MODEL // 2] | (bits[:, D_MODEL // 2:] >> 16)
    for c in range(N_PACKED):
        h2_ref[pl.ds(c, TM, stride=N_PACKED), :] = words[:, LANES * c:LANES * (c + 1)]
    h_lo = (h2 - h_hi.astype(F32)).astype(BF16)
    logits = (jnp.dot(h_hi, wrh_ref[...], preferred_element_type=F32)
              + jnp.dot(h_hi, wrl_ref[...], preferred_element_type=F32)
              + jnp.dot(h_lo, wrh_ref[...], preferred_element_type=F32)
              + br_ref[...])
    lt_scr[...] = logits.T
    sc = []
    for e in range(N_EXPERTS):
        z = lt_scr[e:e + 1, :]
        sc.append(1.0 / (1.0 + jnp.exp(-z)))
    ninf = jnp.float32(-jnp.inf)
    gscore, gtop = [], []
    for g in range(N_GROUPS):
        a4 = sc[EXPERTS_PER_GROUP * g:EXPERTS_PER_GROUP * (g + 1)]
        m1 = jnp.maximum(jnp.maximum(a4[0], a4[1]), jnp.maximum(a4[2], a4[3]))
        i1 = _first_max_index(a4, m1)
        b4 = [jnp.where(i1 == float(j), ninf, a4[j]) for j in range(EXPERTS_PER_GROUP)]
        m2 = jnp.maximum(jnp.maximum(b4[0], b4[1]), jnp.maximum(b4[2], b4[3]))
        i2 = _first_max_index(b4, m2)
        gscore.append(m1 + m2)
        gtop.append((m1, i1, m2, i2))
    gm = jnp.maximum(jnp.maximum(gscore[0], gscore[1]), jnp.maximum(gscore[2], gscore[3]))
    best = _first_max_index(gscore, gm)
    m1 = _select(best, [t[0] for t in gtop])
    i1 = _select(best, [t[1] for t in gtop])
    m2 = _select(best, [t[2] for t in gtop])
    i2 = _select(best, [t[3] for t in gtop])
    den = m1 + m2
    first_low = i1 < i2
    ia = jnp.minimum(i1, i2)
    ib = jnp.maximum(i1, i2)
    pid = jnp.where(ia == 0.0, ib - 1.0, jnp.where(ia == 1.0, 6.0 - ib, 5.0))
    g_low = jnp.where(first_low, m1, m2) / den
    g_high = jnp.where(first_low, m2, m1) / den
    swapped = pid == 5.0
    rt_ref[...] = jnp.zeros(rt_ref.shape, F32)
    rt_ref[0:1, :] = best * float(N_PAIRS) + pid
    rt_ref[1:2, :] = jnp.where(swapped, g_high, g_low)
    rt_ref[2:3, :] = jnp.where(swapped, g_low, g_high)


def _moe_kernel(r_tile, ea_ref, eb_ref, nv_ref,
                gcur_ref, gnext_ref, sprev_ref, scur_ref, gates_ref, h2_hbm,
                wga_ref, wua_ref, wda_ref, wgb_ref, wub_ref, wdb_ref,
                y_hbm, xbuf, ybuf, gsem, ssem):
    t = pl.program_id(0)
    nvalid = nv_ref[0]
    slot = t % 2

    def gather_start(idx_ref, s, rows=None):
        for r in (range(r_tile) if rows is None else rows):
            pltpu.make_async_copy(h2_hbm.at[idx_ref[0, 0, r]],
                                  xbuf.at[s, pl.ds(N_PACKED * r, N_PACKED)], gsem.at[s]).start()

    def gather_wait(s):
        pltpu.make_async_copy(xbuf.at[s], xbuf.at[s], gsem.at[s]).wait()

    def scatter_start(idx_ref, s, rows=None):
        for r in (range(r_tile) if rows is None else rows):
            pltpu.make_async_copy(ybuf.at[s, pl.ds(N_CHUNKS * r, N_CHUNKS)],
                                  y_hbm.at[idx_ref[0, 0, r]], ssem).start()

    def scatter_wait():
        pltpu.make_async_copy(ybuf.at[0], ybuf.at[0], ssem).wait()

    @pl.when(t == 0)
    def _():
        ybuf[1] = jnp.zeros(ybuf.shape[1:], F32)
        gather_start(gcur_ref, 0)

    @pl.when(t < nvalid)
    def _():
        @pl.when(t > 0)
        def _():
            scatter_wait()

        gather_wait(slot)
        xw = jnp.concatenate([xbuf[slot, pl.ds(c, r_tile, stride=N_PACKED), :] for c in range(N_PACKED)],
                             axis=-1)
        x = jnp.concatenate([pltpu.bitcast(xw & jnp.uint32(0xFFFF0000), F32).astype(BF16),
                             pltpu.bitcast(xw << 16, F32).astype(BF16)], axis=-1)
        gts = gates_ref[...]
        n_groups = 2 * (2 * (D_EXPERT // MXU_COLS) + D_MODEL // MXU_COLS)
        per_group = r_tile // n_groups
        groups = iter(range(n_groups))

        def issue_copies():
            k = next(groups)
            rows = range(per_group * k, per_group * (k + 1))
            gather_start(gnext_ref, 1 - slot, rows)
            scatter_start(sprev_ref, 1 - slot, rows)

        y_cols = [None] * (D_MODEL // MXU_COLS)
        for (wg, wu, wd, col) in ((wga_ref, wua_ref, wda_ref, 0), (wgb_ref, wub_ref, wdb_ref, 1)):
            h_cols = []
            for c in range(D_EXPERT // MXU_COLS):
                cs = slice(MXU_COLS * c, MXU_COLS * (c + 1))
                g = jnp.dot(x, wg[0, 0, :, cs].astype(BF16), preferred_element_type=F32)
                issue_copies()
                u = jnp.dot(x, wu[0, 0, :, cs].astype(BF16), preferred_element_type=F32)
                issue_copies()
                h_cols.append((g * (1.0 / (1.0 + jnp.exp(-g))) * u).astype(BF16))
            hmid = jnp.concatenate(h_cols, axis=-1)
            gate = gts[:, col:col + 1]
            for c in range(D_MODEL // MXU_COLS):
                cs = slice(MXU_COLS * c, MXU_COLS * (c + 1))
                ye = gate * jnp.dot(hmid, wd[0, 0, :, cs].astype(BF16), preferred_element_type=F32)
                issue_copies()
                if col == 0:
                    y_cols[c] = ye
                else:
                    yc = y_cols[c] + ye
                    for half in range(MXU_COLS // LANES):
                        ybuf[slot, pl.ds(c * (MXU_COLS // LANES) + half, r_tile, stride=N_CHUNKS), :] = (
                            yc[:, LANES * half:LANES * (half + 1)])
        assert next(groups, None) is None and per_group * n_groups == r_tile

        @pl.when(t == nvalid - 1)
        def _():
            gather_wait(1 - slot)
            scatter_wait()
            scatter_start(scur_ref, slot)
            scatter_wait()


def _moe(h2_slab, rt, wg, wu, wd, layer, n_tok):
    r = R_MOE
    nt = n_tok // r + N_BUCKETS
    bucket = rt[0].astype(jnp.int32)
    n_pad = nt * r - n_tok
    buckets = jnp.arange(N_BUCKETS, dtype=jnp.int32)
    counts = jnp.sum((bucket[:, None] == buckets[None, :]).astype(jnp.int32), axis=0)
    tiles_b = (counts + r - 1) // r
    tile_end = jnp.cumsum(tiles_b)
    nvalid = tile_end[-1]
    pad_end = jnp.cumsum(tiles_b * r - counts)
    pad_bucket = jnp.sum((jnp.arange(n_pad, dtype=jnp.int32)[:, None] >= pad_end[None, :]).astype(jnp.int32),
                         axis=1)
    keys = jnp.concatenate([2 * bucket, 2 * pad_bucket + 1])
    zi = jnp.zeros((n_pad,), jnp.int32)
    zf = jnp.zeros((n_pad,), F32)
    _, gidx, valid, gate_a, gate_b = lax.sort(
        (keys, jnp.concatenate([jnp.arange(n_tok, dtype=jnp.int32), zi]),
         jnp.concatenate([jnp.ones((n_tok,), jnp.int32), zi]),
         jnp.concatenate([rt[1], zf]), jnp.concatenate([rt[2], zf])),
        num_keys=1, is_stable=True)
    rows = jnp.arange(r, dtype=jnp.int32)[None, :]
    spare = jnp.broadcast_to(n_tok + rows, (nt, r))
    sidx = jnp.concatenate([spare[:1], jnp.where(valid.reshape(nt, r) == 1, gidx.reshape(nt, r), spare)], axis=0)
    gates = jnp.stack([gate_a, gate_b], axis=-1)
    tix = jnp.minimum(jnp.arange(nt, dtype=jnp.int32), nvalid - 1)
    tb = jnp.minimum(jnp.sum((tix[:, None] >= tile_end[None, :]).astype(jnp.int32), axis=1), N_BUCKETS - 1)
    grp = tb // N_PAIRS
    ea = (grp * EXPERTS_PER_GROUP + jnp.asarray(PAIR_A, jnp.int32)[tb % N_PAIRS]).astype(jnp.int32)
    eb = (grp * EXPERTS_PER_GROUP + jnp.asarray(PAIR_B, jnp.int32)[tb % N_PAIRS]).astype(jnp.int32)
    nv = nvalid.reshape(1).astype(jnp.int32)
    gidx3 = gidx.reshape(nt, 1, r)
    sidx3 = sidx.reshape(nt + 1, 1, r)

    def smem(index):
        return pl.BlockSpec((1, 1, r), lambda t, ea, eb, nv: (index(t), 0, 0), memory_space=pltpu.SMEM)

    w_in_a = pl.BlockSpec((1, 1, D_MODEL, D_EXPERT), lambda t, ea, eb, nv: (layer, ea[t], 0, 0))
    w_out_a = pl.BlockSpec((1, 1, D_EXPERT, D_MODEL), lambda t, ea, eb, nv: (layer, ea[t], 0, 0))
    w_in_b = pl.BlockSpec((1, 1, D_MODEL, D_EXPERT), lambda t, ea, eb, nv: (layer, eb[t], 0, 0))
    w_out_b = pl.BlockSpec((1, 1, D_EXPERT, D_MODEL), lambda t, ea, eb, nv: (layer, eb[t], 0, 0))
    grid_spec = pltpu.PrefetchScalarGridSpec(
        num_scalar_prefetch=3,
        grid=(nt,),
        in_specs=[smem(lambda t: t), smem(lambda t: jnp.minimum(t + 1, nt - 1)),
                  smem(lambda t: t), smem(lambda t: t + 1),
                  pl.BlockSpec((r, 2), lambda t, ea, eb, nv: (t, 0)),
                  pl.BlockSpec(memory_space=pl.ANY),
                  w_in_a, w_in_a, w_out_a, w_in_b, w_in_b, w_out_b],
        out_specs=pl.BlockSpec(memory_space=pl.ANY),
        scratch_shapes=[pltpu.VMEM((2, r * N_PACKED, LANES), jnp.uint32),
                        pltpu.VMEM((2, r * N_CHUNKS, LANES), F32),
                        pltpu.SemaphoreType.DMA((2,)),
                        pltpu.SemaphoreType.DMA],
    )
    y = pl.pallas_call(
        functools.partial(_moe_kernel, r),
        grid_spec=grid_spec,
        out_shape=jax.ShapeDtypeStruct((n_tok + r, N_CHUNKS, LANES), F32),
        compiler_params=_cparams(1),
        name="moe_experts",
    )(ea, eb, nv, gidx3, gidx3, sidx3, sidx3, gates, h2_slab, wg, wu, wd, wg, wu, wd)
    return y


def _final_kernel(npt, x1_ref, y_ref, g2_ref, gf_ref, op_ref, os_ref):
    i = pl.program_id(0)
    x = x1_ref[...] + g2_ref[0] * _slab_to_nat(y_ref, TM)
    ms = jnp.mean(x * x, axis=-1, keepdims=True)
    out = x * lax.rsqrt(ms + EPS) * gf_ref[...]

    @pl.when(i < npt)
    def _():
        op_ref[...] = out

    @pl.when(i >= npt)
    def _():
        os_ref[...] = out


def _rope_tables(dseq):
    n_freq = HEAD_DIM // 4
    inv = ROPE_THETA ** (-jnp.arange(0, 2 * n_freq, 2, dtype=F32) / (2 * n_freq))
    pos = jnp.arange(dseq)
    row = (pos // GRID_W).astype(F32)
    col = (pos % GRID_W).astype(F32)
    ang_r = row[:, None] * inv
    ang_c = col[:, None] * inv
    ang = jnp.concatenate([ang_r, ang_r, ang_c, ang_c], axis=-1)
    sign = jnp.tile(jnp.concatenate([-jnp.ones((n_freq,), F32), jnp.ones((n_freq,), F32)]), 2)
    cos = jnp.tile(jnp.cos(ang), (1, 2))
    sin = jnp.tile(jnp.sin(ang) * sign, (1, 2))
    cos = jnp.concatenate([jnp.ones((TM, LANES), F32), cos], axis=0)
    sin = jnp.concatenate([jnp.zeros((TM, LANES), F32), sin], axis=0)
    return cos, sin


def _dup_halves(w, n_heads):
    lead = w.shape[:-1]
    w = w.reshape(lead + (n_heads, 1, HEAD_DIM))
    w = jnp.broadcast_to(w, lead + (n_heads, 2, HEAD_DIM))
    return w.reshape(lead + (n_heads * 2 * HEAD_DIM,))


def kernel(x_prompt, x_sample, cache_k_a, cache_v_a, cache_k_b, cache_v_b, c, c_ctx, w_mod, b_mod, g_attn, g_ffn, w_qkv_a, w_o_a, g_q_a, g_k_a, w_qkv_b, w_o_b, lam_q1, lam_k1, lam_q2, lam_k2, g_sub_b, w_router, b_router, w_gate, w_up, w_down, g_final):
    nbp, seq, d = x_prompt.shape
    nbs, dseq, _ = x_sample.shape
    depth = w_mod.shape[0]
    assert d == D_MODEL and seq % TM == 0 and dseq % TM == 0
    tp, ts = nbp * seq, nbs * dseq
    t_all = tp + ts
    npt = tp // TM
    tpb = dseq // TM
    nti = t_all // TM
    assert nbs + 1 <= N_MOD_ROWS

    xp = x_prompt.reshape(tp, d)
    xs = x_sample.reshape(ts, d)

    cvec = jnp.zeros((N_MOD_ROWS, d), F32).at[0].set(c_ctx).at[1:1 + nbs].set(c)
    mod = _modulation(cvec, w_mod, b_mod).reshape(depth * N_MOD_ROWS * N_MOD, 1, d)

    cos_t, sin_t = _rope_tables(dseq)
    qscale = HEAD_DIM ** -0.5 * math.log2(math.e)

    nqk = D_MODEL + 2 * A_KV_HEADS * HEAD_DIM
    lane_id = jnp.arange(nqk)
    seg = jnp.where(lane_id < D_MODEL, lane_id // HEAD_DIM, A_HEADS + (lane_id - D_MODEL) // (2 * HEAD_DIM))
    seg_w = jnp.where(lane_id < D_MODEL, 1.0 / HEAD_DIM, 1.0 / (2 * HEAD_DIM))
    onehot = (seg[:, None] == jnp.arange(LANES)[None, :]).astype(F32)
    eq = (onehot * seg_w[:, None]).astype(BF16)
    et = jnp.concatenate([onehot.T, onehot.T], axis=0).astype(BF16)

    w_router_p = jnp.zeros((d, LANES), F32).at[:, :N_EXPERTS].set(w_router)
    wr_hi = w_router_p.astype(BF16)
    wr_lo = (w_router_p - wr_hi.astype(F32)).astype(BF16)
    br = jnp.zeros((1, LANES), F32).at[0, :N_EXPERTS].set(b_router)

    past = cache_k_a.shape[2]
    ck_a = _dup_halves(cache_k_a.reshape(nbs, -1, past, A_KV_HEADS * HEAD_DIM), A_KV_HEADS).astype(BF16)
    cv_a = _dup_halves(cache_v_a.reshape(nbs, -1, past, A_KV_HEADS * HEAD_DIM), A_KV_HEADS).astype(BF16)
    ck_b = cache_k_b.reshape(nbs, -1, past, D_MODEL).astype(BF16)
    cv_b = cache_v_b.reshape(nbs, -1, past, D_MODEL).astype(BF16)

    tok_spec = pl.BlockSpec((TM, D_MODEL), lambda i: (i, 0))
    tiles_per_seq = seq // TM

    def state_spec(w, j):
        def index(i):
            ii = jnp.minimum(i, npt - 1)
            return (ii // tiles_per_seq, j, ii % tiles_per_seq, 0)
        return pl.BlockSpec((1, 1, TM, w), index)

    any_spec = pl.BlockSpec(memory_space=pl.ANY)
    n_a, n_b = (depth + 1) // 2, depth // 2
    wkv = 2 * A_KV_HEADS * HEAD_DIM
    state_k_a = jnp.zeros((nbp, n_a, seq, wkv // 2), F32)
    state_v_a = jnp.zeros((nbp, n_a, seq, wkv // 2), F32)
    state_k_b = jnp.zeros((nbp, n_b, seq, d), F32)
    state_v_b = jnp.zeros((nbp, n_b, seq, d), F32)
    prev = None
    for l in range(depth):
        j = l // 2
        has_prev = prev is not None
        if has_prev:
            x1_prev, y_prev, l_prev = prev
            x_args = [x1_prev, y_prev, mod]
            x_specs = _x_specs(True, npt)
            x_specs[2] = _mod_spec(l_prev, 5, npt, tpb)
        else:
            x_args = [xp, xs]
            x_specs = _x_specs(False, npt)
        mod_args = lambda ks: [mod] * len(ks)
        mod_specs = lambda ks: [_mod_spec(l, k, npt, tpb) for k in ks]
        g_attn_l = g_attn[l].reshape(1, d)

        if l % 2 == 0:
            wq, wk, wv = jnp.split(w_qkv_a[j], [A_HEADS * HEAD_DIM, (A_HEADS + A_KV_HEADS) * HEAD_DIM], axis=-1)
            w_all = jnp.concatenate([wq, _dup_halves(wk, A_KV_HEADS), _dup_halves(wv, A_KV_HEADS)], axis=-1).astype(BF16)
            gvec = jnp.concatenate([jnp.tile(g_q_a[j], A_HEADS) * qscale,
                                    jnp.tile(g_k_a[j], 2 * A_KV_HEADS)]).reshape(1, nqk)
            n_in = len(x_specs) + 11
            q, k, v, state_k_a, state_v_a = pl.pallas_call(
                functools.partial(_proj_a_kernel, has_prev, npt),
                grid=(nti,),
                in_specs=x_specs + mod_specs((0, 1)) + [
                    _const_spec((1, d)), _const_spec(w_all.shape), _const_spec(eq.shape),
                    _const_spec(et.shape), _const_spec((1, nqk)), _rope_spec(npt, tpb), _rope_spec(npt, tpb),
                    any_spec, any_spec],
                out_specs=[tok_spec, pl.BlockSpec((TM, wkv), lambda i: (i, 0)),
                           pl.BlockSpec((TM, wkv), lambda i: (i, 0)),
                           state_spec(wkv // 2, j), state_spec(wkv // 2, j)],
                out_shape=[jax.ShapeDtypeStruct((t_all, d), BF16),
                           jax.ShapeDtypeStruct((t_all, wkv), BF16),
                           jax.ShapeDtypeStruct((t_all, wkv), BF16),
                           jax.ShapeDtypeStruct(state_k_a.shape, F32),
                           jax.ShapeDtypeStruct(state_v_a.shape, F32)],
                input_output_aliases={n_in - 2: 3, n_in - 1: 4},
                compiler_params=_cparams(1),
                name="proj_a",
            )(*x_args, *mod_args((0, 1)), g_attn_l, w_all, eq, et, gvec, cos_t, sin_t, state_k_a, state_v_a)
            o_p, o_s = _attention("a", q, k, v, nbp, seq, nbs, dseq, ck_a, cv_a, j)
            w_o = w_o_a[j].astype(BF16)
        else:
            lam_init = 0.8 - 0.6 * math.exp(-0.3 * l)
            lam = (jnp.exp(jnp.sum((lam_q1[j] * lam_k1[j]).astype(F32)))
                   - jnp.exp(jnp.sum((lam_q2[j] * lam_k2[j]).astype(F32))) + lam_init).reshape(1, 1)
            wq, wk, wv = jnp.split(w_qkv_b[j], 3, axis=-1)
            w_all = jnp.concatenate([wq * qscale, wk, wv], axis=-1).astype(BF16)
            n_in = len(x_specs) + 8
            q, k, v, state_k_b, state_v_b = pl.pallas_call(
                functools.partial(_proj_b_kernel, has_prev, npt),
                grid=(nti,),
                in_specs=x_specs + mod_specs((0, 1)) + [
                    _const_spec((1, d)), _const_spec(w_all.shape), _rope_spec(npt, tpb), _rope_spec(npt, tpb),
                    any_spec, any_spec],
                out_specs=[tok_spec, tok_spec, tok_spec, state_spec(d, j), state_spec(d, j)],
                out_shape=[jax.ShapeDtypeStruct((t_all, d), BF16),
                           jax.ShapeDtypeStruct((t_all, d), BF16),
                           jax.ShapeDtypeStruct((t_all, d), BF16),
                           jax.ShapeDtypeStruct(state_k_b.shape, F32),
                           jax.ShapeDtypeStruct(state_v_b.shape, F32)],
                input_output_aliases={n_in - 2: 3, n_in - 1: 4},
                compiler_params=_cparams(1),
                name="proj_b",
            )(*x_args, *mod_args((0, 1)), g_attn_l, w_all, cos_t, sin_t, state_k_b, state_v_b)
            gsub = (g_sub_b[j] * (1.0 - lam_init)).reshape(1, 2 * HEAD_DIM)
            o_p, o_s = _attention("b", q, k, v, nbp, seq, nbs, dseq, ck_b, cv_b, j, lam=lam, gsub=gsub)
            w_o = w_o_b[j].astype(BF16)

        x1, h2, rt = pl.pallas_call(
            functools.partial(_kb_kernel, has_prev, npt),
            grid=(nti,),
            in_specs=x_specs + [
                pl.BlockSpec((TM, d), lambda i: (jnp.minimum(i, npt - 1), 0)),
                pl.BlockSpec((TM, d), lambda i: (jnp.maximum(i - npt, 0), 0))] + mod_specs((2, 3, 4)) + [
                _const_spec((d, d)), _const_spec((1, d)), _const_spec((d, LANES)), _const_spec((d, LANES)),
                _const_spec((1, LANES))],
            out_specs=[tok_spec, pl.BlockSpec((TM * N_PACKED, LANES), lambda i: (i, 0)),
                       pl.BlockSpec((8, TM), lambda i: (0, i))],
            out_shape=[jax.ShapeDtypeStruct((t_all, d), F32),
                       jax.ShapeDtypeStruct((t_all * N_PACKED, LANES), jnp.uint32),
                       jax.ShapeDtypeStruct((8, t_all), F32)],
            scratch_shapes=[pltpu.VMEM((LANES, TM), F32)],
            compiler_params=_cparams(1),
            name="oproj_router",
        )(*x_args, o_p, o_s, *mod_args((2, 3, 4)), w_o, g_ffn[l].reshape(1, d), wr_hi, wr_lo, br)

        y = _moe(h2.reshape(t_all, N_PACKED, LANES), rt, w_gate, w_up, w_down, l, t_all)
        prev = (x1, y.reshape((t_all + R_MOE) * N_CHUNKS, LANES), l)

    x1_prev, y_prev, l_prev = prev
    y_p, y_s = pl.pallas_call(
        functools.partial(_final_kernel, npt),
        grid=(nti,),
        in_specs=[tok_spec, pl.BlockSpec((TM * N_CHUNKS, LANES), lambda i: (i, 0)),
                  _mod_spec(l_prev, 5, npt, tpb), _const_spec((1, d))],
        out_specs=[pl.BlockSpec((TM, d), lambda i: (jnp.minimum(i, npt - 1), 0)),
                   pl.BlockSpec((TM, d), lambda i: (jnp.maximum(i - npt, 0), 0))],
        out_shape=[jax.ShapeDtypeStruct((tp, d), F32), jax.ShapeDtypeStruct((ts, d), F32)],
        compiler_params=_cparams(1),
        name="final_norm",
    )(x1_prev, y_prev, mod, g_final.reshape(1, d))

    return (y_p.reshape(nbp, seq, d), y_s.reshape(nbs, dseq, d),
            state_k_a.reshape(nbp, n_a, seq, A_KV_HEADS, HEAD_DIM),
            state_v_a.reshape(nbp, n_a, seq, A_KV_HEADS, HEAD_DIM),
            state_k_b.reshape(nbp, n_b, seq, B_HEADS, 2, HEAD_DIM),
            state_v_b.reshape(nbp, n_b, seq, B_HEADS, 2 * HEAD_DIM))
```

```python
import functools
import math

import jax
import jax.numpy as jnp
from jax import lax
from jax.experimental import pallas as pl
from jax.experimental.pallas import tpu as pltpu

F32 = jnp.float32
BF16 = jnp.bfloat16

D_MODEL = 1024
HEAD_DIM = 64
A_HEADS = 16
A_KV_HEADS = 4
B_HEADS = 8
N_EXPERTS = 16
N_GROUPS = 4
EXPERTS_PER_GROUP = 4
D_EXPERT = 512
N_MOD = 6
N_MOD_ROWS = 16
ROPE_THETA = 10000.0
GRID_W = 64
EPS = 1e-6

LANES = 128
N_CHUNKS = D_MODEL // LANES
TM = 256
TQ_A = 256
TQ_B = 512
R_MOE = 256
N_PAIRS = 6
N_BUCKETS = N_GROUPS * N_PAIRS
PAIR_A = (0, 0, 0, 1, 1, 3)
PAIR_B = (1, 2, 3, 3, 2, 2)
VMEM_LIMIT = 56 * 1024 * 1024


def _cparams(n_axes=1):
    return pltpu.CompilerParams(dimension_semantics=("arbitrary",) * n_axes,
                                vmem_limit_bytes=VMEM_LIMIT)


def _slab_to_nat(ref, tm):
    return jnp.concatenate([ref[pl.ds(c, tm, stride=N_CHUNKS), :] for c in range(N_CHUNKS)], axis=-1)


def _nat_to_slab(ref, val, tm):
    for c in range(N_CHUNKS):
        ref[pl.ds(c, tm, stride=N_CHUNKS), :] = val[:, LANES * c:LANES * (c + 1)]


def _norm_mod(x, g, shift, scale):
    ms = jnp.mean(x * x, axis=-1, keepdims=True)
    return x * lax.rsqrt(ms + EPS) * g * (1.0 + scale) + shift


def _rope_cols(x, cos, sin, second_of_pair):
    outs = []
    for c in range(x.shape[1] // LANES):
        xc = x[:, LANES * c:LANES * (c + 1)]
        partner = jnp.where(second_of_pair, pltpu.roll(xc, 16, 1), pltpu.roll(xc, LANES - 16, 1))
        outs.append(xc * cos + partner * sin)
    return jnp.concatenate(outs, axis=-1)


def _stream_x(i, npt, refs, has_prev, tm):
    if has_prev:
        x1_ref, y_ref, g2_ref = refs
        return x1_ref[...] + g2_ref[0] * _slab_to_nat(y_ref, tm)
    xp_ref, xs_ref = refs
    return jnp.where(i < npt, xp_ref[...], xs_ref[...])


def _mod_kernel(cv_ref, w_ref, b_ref, o_ref):
    cv = cv_ref[...]
    s = cv * (1.0 / (1.0 + jnp.exp(-cv)))
    o_ref[0] = jnp.dot(s.astype(BF16), w_ref[0].astype(BF16), preferred_element_type=F32) + b_ref[0]


def _modulation(cvec, w_mod, b_mod):
    depth, d, n = w_mod.shape
    tn = 1024
    return pl.pallas_call(
        _mod_kernel,
        grid=(depth, n // tn),
        in_specs=[pl.BlockSpec((N_MOD_ROWS, d), lambda l, j: (0, 0)),
                  pl.BlockSpec((1, d, tn), lambda l, j: (l, 0, j)),
                  pl.BlockSpec((1, 1, tn), lambda l, j: (l, 0, j))],
        out_specs=pl.BlockSpec((1, N_MOD_ROWS, tn), lambda l, j: (l, 0, j)),
        out_shape=jax.ShapeDtypeStruct((depth, N_MOD_ROWS, n), F32),
        compiler_params=_cparams(2),
        name="modulation",
    )(cvec, w_mod, b_mod.reshape(depth, 1, n))


def _proj_a_kernel(has_prev, npt, *refs):
    n_x = 3 if has_prev else 2
    xrefs, refs = refs[:n_x], refs[n_x:]
    (sh_ref, sc_ref, ga_ref, w_ref, eq_ref, et_ref, gv_ref, cos_ref, sin_ref, _, _,
     q_ref, k_ref, v_ref, ks_ref, vs_ref) = refs
    i = pl.program_id(0)
    x = _stream_x(i, npt, xrefs, has_prev, TM)
    hb = _norm_mod(x, ga_ref[...], sh_ref[0], sc_ref[0]).astype(BF16)
    qkv = jnp.dot(hb, w_ref[...], preferred_element_type=F32)
    nqk = D_MODEL + 2 * A_KV_HEADS * HEAD_DIM
    qk = qkv[:, :nqk]
    ms = jnp.dot((qk * qk).astype(BF16), eq_ref[...], preferred_element_type=F32)
    inv = lax.rsqrt(ms + EPS)
    inv_hi = inv.astype(BF16)
    inv_lo = (inv - inv_hi.astype(F32)).astype(BF16)
    invf = jnp.dot(jnp.concatenate([inv_hi, inv_lo], axis=-1), et_ref[...], preferred_element_type=F32)
    qkn = qk * invf * gv_ref[...]
    lane = lax.broadcasted_iota(jnp.int32, (1, LANES), 1)
    second = (lane & 16) != 0
    lo = lane < HEAD_DIM
    qkr = _rope_cols(qkn, cos_ref[...], sin_ref[...], second)
    q_ref[...] = qkr[:, :D_MODEL].astype(BF16)
    k_ref[...] = qkr[:, D_MODEL:].astype(BF16)
    vd = qkv[:, nqk:]
    v_ref[...] = vd.astype(BF16)

    @pl.when(i < npt)
    def _():
        kd = qkn[:, D_MODEL:]
        for j in range(2):
            ks_ref[0, 0, :, LANES * j:LANES * (j + 1)] = jnp.where(
                lo, kd[:, LANES * 2 * j:LANES * (2 * j + 1)], kd[:, LANES * (2 * j + 1):LANES * (2 * j + 2)])
            vs_ref[0, 0, :, LANES * j:LANES * (j + 1)] = jnp.where(
                lo, vd[:, LANES * 2 * j:LANES * (2 * j + 1)], vd[:, LANES * (2 * j + 1):LANES * (2 * j + 2)])


def _proj_b_kernel(has_prev, npt, *refs):
    n_x = 3 if has_prev else 2
    xrefs, refs = refs[:n_x], refs[n_x:]
    (sh_ref, sc_ref, ga_ref, w_ref, cos_ref, sin_ref, _, _,
     q_ref, k_ref, v_ref, ks_ref, vs_ref) = refs
    i = pl.program_id(0)
    x = _stream_x(i, npt, xrefs, has_prev, TM)
    hb = _norm_mod(x, ga_ref[...], sh_ref[0], sc_ref[0]).astype(BF16)
    qkv = jnp.dot(hb, w_ref[...], preferred_element_type=F32)
    lane = lax.broadcasted_iota(jnp.int32, (1, LANES), 1)
    second = (lane & 16) != 0
    qk = qkv[:, :2 * D_MODEL]
    qkr = _rope_cols(qk, cos_ref[...], sin_ref[...], second)
    q_ref[...] = qkr[:, :D_MODEL].astype(BF16)
    k_ref[...] = qkr[:, D_MODEL:].astype(BF16)
    v = qkv[:, 2 * D_MODEL:]
    v_ref[...] = v.astype(BF16)

    @pl.when(i < npt)
    def _():
        ks_ref[0, 0] = qk[:, D_MODEL:]
        vs_ref[0, 0] = v


def _x_specs(has_prev, npt):
    if has_prev:
        return [pl.BlockSpec((TM, D_MODEL), lambda i: (i, 0)),
                pl.BlockSpec((TM * N_CHUNKS, LANES), lambda i: (i, 0)),
                None]
    return [pl.BlockSpec((TM, D_MODEL), lambda i: (jnp.minimum(i, npt - 1), 0)),
            pl.BlockSpec((TM, D_MODEL), lambda i: (jnp.maximum(i - npt, 0), 0))]


def _mod_spec(layer, k, npt, tpb):
    def index(i):
        row = jnp.where(i < npt, 0, 1 + (i - npt) // tpb)
        return ((layer * N_MOD_ROWS + row) * N_MOD + k, 0, 0)
    return pl.BlockSpec((1, 1, D_MODEL), index)


def _rope_spec(npt, tpb):
    return pl.BlockSpec((TM, LANES), lambda i: (jnp.where(i < npt, 0, 1 + (i - npt) % tpb), 0))


def _const_spec(shape):
    nd = len(shape)
    return pl.BlockSpec(shape, lambda i: (0,) * nd)


def _lane_masks():
    lane = lax.broadcasted_iota(jnp.int32, (1, LANES), 1)
    lo = lane < HEAD_DIM
    mlo = jnp.where(lo, 1.0, 0.0).astype(BF16)
    mhi = jnp.where(lo, 0.0, 1.0).astype(BF16)
    return lo, mlo, mhi


_NT_DIMS = (((1,), (1,)), ((), ()))


def _scores_stage(lhs, cols, lk, has_cache, k_ref, ck_ref, s_scr, m_scr):
    s = lax.dot_general(lhs, k_ref[:, cols], _NT_DIMS, preferred_element_type=F32)
    s_scr[:, 0:lk] = s
    mp = s[:, 0:LANES]
    for c in range(1, lk // LANES):
        mp = jnp.maximum(mp, s[:, LANES * c:LANES * (c + 1)])
    if has_cache:
        s2 = lax.dot_general(lhs, ck_ref[0, 0, :, cols], _NT_DIMS, preferred_element_type=F32)
        past = s2.shape[1]
        s_scr[:, lk:lk + past] = s2
        for c in range(past // LANES):
            mp = jnp.maximum(mp, s2[:, LANES * c:LANES * (c + 1)])
    m_scr[...] = mp


def _weighted_values(s_scr, m_scr, lk, has_cache, v_ref, cv_ref, cols):
    m = jnp.max(m_scr[...], axis=-1, keepdims=True)
    pb = jnp.exp2(s_scr[...] - m).astype(BF16)
    rhs = jnp.concatenate([v_ref[:, cols], jnp.ones((lk, LANES), BF16)], axis=-1)
    res = jnp.dot(pb[:, 0:lk], rhs, preferred_element_type=F32)
    if has_cache:
        cv = cv_ref[0, 0, :, cols]
        rhs2 = jnp.concatenate([cv, jnp.ones(cv.shape, BF16)], axis=-1)
        res = res + jnp.dot(pb[:, lk:], rhs2, preferred_element_type=F32)
    return res


def _attn_a_kernel(tq, has_cache, *refs):
    if has_cache:
        q_ref, k_ref, v_ref, ck_ref, cv_ref, o_ref, s0, s1, m0, m1 = refs
    else:
        q_ref, k_ref, v_ref, o_ref, s0, s1, m0, m1 = refs
        ck_ref = cv_ref = None
    lk = k_ref.shape[0]
    lo, mlo, mhi = _lane_masks()
    bufs = ((s0, m0), (s1, m1))

    def scores(g, s_scr, m_scr):
        qa = q_ref[:, LANES * 2 * g:LANES * (2 * g + 1)]
        qb = q_ref[:, LANES * (2 * g + 1):LANES * (2 * g + 2)]
        lhs = jnp.concatenate([qa * mlo, qa * mhi, qb * mlo, qb * mhi], axis=0)
        _scores_stage(lhs, slice(LANES * g, LANES * (g + 1)), lk, has_cache, k_ref, ck_ref, s_scr, m_scr)

    def finish(g, s_scr, m_scr):
        cols = slice(LANES * g, LANES * (g + 1))
        res = _weighted_values(s_scr, m_scr, lk, has_cache, v_ref, cv_ref, cols)
        o = res[:, 0:LANES] * (1.0 / res[:, LANES:])
        o_ref[:, LANES * 2 * g:LANES * (2 * g + 1)] = jnp.where(lo, o[0:tq], o[tq:2 * tq]).astype(BF16)
        o_ref[:, LANES * (2 * g + 1):LANES * (2 * g + 2)] = jnp.where(
            lo, o[2 * tq:3 * tq], o[3 * tq:4 * tq]).astype(BF16)

    scores(0, *bufs[0])
    for g in range(A_KV_HEADS):
        if g + 1 < A_KV_HEADS:
            scores(g + 1, *bufs[(g + 1) % 2])
        finish(g, *bufs[g % 2])


def _attn_b_kernel(tq, has_cache, *refs):
    if has_cache:
        lam_ref, q_ref, k_ref, v_ref, ck_ref, cv_ref, gs_ref, o_ref, s0, s1, m0, m1 = refs
    else:
        lam_ref, q_ref, k_ref, v_ref, gs_ref, o_ref, s0, s1, m0, m1 = refs
        ck_ref = cv_ref = None
    lk = k_ref.shape[0]
    lam = lam_ref[0, 0]
    _, mlo, mhi = _lane_masks()
    bufs = ((s0, m0), (s1, m1))

    def scores(h, s_scr, m_scr):
        cols = slice(LANES * h, LANES * (h + 1))
        qc = q_ref[:, cols]
        lhs = jnp.concatenate([qc * mlo, qc * mhi], axis=0)
        _scores_stage(lhs, cols, lk, has_cache, k_ref, ck_ref, s_scr, m_scr)

    def finish(h, s_scr, m_scr):
        cols = slice(LANES * h, LANES * (h + 1))
        res = _weighted_values(s_scr, m_scr, lk, has_cache, v_ref, cv_ref, cols)
        on = res[:, 0:LANES] * (1.0 / res[:, LANES:])
        o = on[0:tq] - lam * on[tq:2 * tq]
        ms = jnp.mean(o * o, axis=-1, keepdims=True)
        o_ref[:, cols] = (o * lax.rsqrt(ms + EPS) * gs_ref[...]).astype(BF16)

    scores(0, *bufs[0])
    for h in range(B_HEADS):
        if h + 1 < B_HEADS:
            scores(h + 1, *bufs[(h + 1) % 2])
        finish(h, *bufs[h % 2])


def _attention(kind, tq, q, k, v, nbp, seq, nbs, dseq, cache_k, cache_v, layer_j, lam=None, gsub=None):
    t_all = q.shape[0]
    wk = k.shape[1]
    kern = _attn_a_kernel if kind == "a" else _attn_b_kernel
    pre_args, pre_specs_1, pre_specs_2 = [], [], []
    post_args, post_specs_1, post_specs_2 = [], [], []
    if kind == "b":
        pre_args = [lam]
        pre_specs_1 = [pl.BlockSpec(memory_space=pltpu.SMEM)]
        pre_specs_2 = [pl.BlockSpec(memory_space=pltpu.SMEM)]
        post_args = [gsub]
        post_specs_1 = [pl.BlockSpec((1, LANES), lambda b: (0, 0))]
        post_specs_2 = [pl.BlockSpec((1, LANES), lambda b, j: (0, 0))]

    stack = 4 if kind == "a" else 2

    def scratch(rows, keys):
        return [pltpu.VMEM((rows, keys), F32), pltpu.VMEM((rows, keys), F32),
                pltpu.VMEM((rows, LANES), F32), pltpu.VMEM((rows, LANES), F32)]

    tqp = min(tq, seq)
    nqp = seq // tqp
    out_p = pl.pallas_call(
        functools.partial(kern, tqp, False),
        grid=(nbp * nqp,),
        in_specs=pre_specs_1 + [
            pl.BlockSpec((tqp, D_MODEL), lambda b: (b, 0)),
            pl.BlockSpec((seq, wk), lambda b: (b // nqp, 0)),
            pl.BlockSpec((seq, wk), lambda b: (b // nqp, 0))] + post_specs_1,
        out_specs=pl.BlockSpec((tqp, D_MODEL), lambda b: (b, 0)),
        out_shape=jax.ShapeDtypeStruct((nbp * seq, D_MODEL), BF16),
        scratch_shapes=scratch(stack * tqp, seq),
        compiler_params=_cparams(1),
        name="attn_%s_ctx" % kind,
    )(*pre_args, q, k, v, *post_args)

    nq = dseq // tq
    qoff = (nbp * seq) // tq
    koff = (nbp * seq) // dseq
    assert (nbp * seq) % dseq == 0
    past = cache_k.shape[2]
    out_s = pl.pallas_call(
        functools.partial(kern, tq, True),
        grid=(nbs, nq),
        in_specs=pre_specs_2 + [
            pl.BlockSpec((tq, D_MODEL), lambda b, j: (qoff + b * nq + j, 0)),
            pl.BlockSpec((dseq, wk), lambda b, j: (koff + b, 0)),
            pl.BlockSpec((dseq, wk), lambda b, j: (koff + b, 0)),
            pl.BlockSpec((1, 1, past, wk), lambda b, j: (b, layer_j, 0, 0)),
            pl.BlockSpec((1, 1, past, wk), lambda b, j: (b, layer_j, 0, 0))] + post_specs_2,
        out_specs=pl.BlockSpec((tq, D_MODEL), lambda b, j: (b * nq + j, 0)),
        out_shape=jax.ShapeDtypeStruct((nbs * dseq, D_MODEL), BF16),
        scratch_shapes=scratch(stack * tq, dseq + past),
        compiler_params=_cparams(2),
        name="attn_%s_lat" % kind,
    )(*pre_args, q, k, v, cache_k, cache_v, *post_args)
    del t_all
    return out_p, out_s


def _first_max_index(vals, m):
    idx = jnp.full_like(m, float(len(vals) - 1))
    for j in range(len(vals) - 2, -1, -1):
        idx = jnp.where(vals[j] == m, float(j), idx)
    return idx


def _select(idx, vals):
    out = vals[-1]
    for j in range(len(vals) - 2, -1, -1):
        out = jnp.where(idx == float(j), vals[j], out)
    return out


def _kb_kernel(has_prev, npt, *refs):
    n_x = 3 if has_prev else 2
    xrefs, refs = refs[:n_x], refs[n_x:]
    (op_ref, os_ref, g1_ref, sh_ref, sc_ref, wo_ref, gf_ref, wrh_ref, wrl_ref, br_ref,
     x1_ref, h2_ref, rt_ref, lt_scr) = refs
    i = pl.program_id(0)
    x = _stream_x(i, npt, xrefs, has_prev, TM)
    o = jnp.where(i < npt, op_ref[...], os_ref[...])
    a = jnp.dot(o, wo_ref[...], preferred_element_type=F32)
    x1 = x + g1_ref[0] * a
    x1_ref[...] = x1
    h2 = _norm_mod(x1, gf_ref[...], sh_ref[0], sc_ref[0])
    _nat_to_slab(h2_ref, h2, TM)
    h_hi = h2.astype(BF16)
    h_lo = (h2 - h_hi.astype(F32)).astype(BF16)
    logits = (jnp.dot(h_hi, wrh_ref[...], preferred_element_type=F32)
              + jnp.dot(h_hi, wrl_ref[...], preferred_element_type=F32)
              + jnp.dot(h_lo, wrh_ref[...], preferred_element_type=F32)
              + br_ref[...])
    lt_scr[...] = logits.T
    sc = []
    for e in range(N_EXPERTS):
        z = lt_scr[e:e + 1, :]
        sc.append(1.0 / (1.0 + jnp.exp(-z)))
    ninf = jnp.float32(-jnp.inf)
    gscore, gtop = [], []
    for g in range(N_GROUPS):
        a4 = sc[EXPERTS_PER_GROUP * g:EXPERTS_PER_GROUP * (g + 1)]
        m1 = jnp.maximum(jnp.maximum(a4[0], a4[1]), jnp.maximum(a4[2], a4[3]))
        i1 = _first_max_index(a4, m1)
        b4 = [jnp.where(i1 == float(j), ninf, a4[j]) for j in range(EXPERTS_PER_GROUP)]
        m2 = jnp.maximum(jnp.maximum(b4[0], b4[1]), jnp.maximum(b4[2], b4[3]))
        i2 = _first_max_index(b4, m2)
        gscore.append(m1 + m2)
        gtop.append((m1, i1, m2, i2))
    gm = jnp.maximum(jnp.maximum(gscore[0], gscore[1]), jnp.maximum(gscore[2], gscore[3]))
    best = _first_max_index(gscore, gm)
    m1 = _select(best, [t[0] for t in gtop])
    i1 = _select(best, [t[1] for t in gtop])
    m2 = _select(best, [t[2] for t in gtop])
    i2 = _select(best, [t[3] for t in gtop])
    den = m1 + m2
    first_low = i1 < i2
    ia = jnp.minimum(i1, i2)
    ib = jnp.maximum(i1, i2)
    pid = jnp.where(ia == 0.0, ib - 1.0, jnp.where(ia == 1.0, 6.0 - ib, 5.0))
    g_low = jnp.where(first_low, m1, m2) / den
    g_high = jnp.where(first_low, m2, m1) / den
    swapped = pid == 5.0
    rt_ref[...] = jnp.zeros(rt_ref.shape, F32)
    rt_ref[0:1, :] = best * float(N_PAIRS) + pid
    rt_ref[1:2, :] = jnp.where(swapped, g_high, g_low)
    rt_ref[2:3, :] = jnp.where(swapped, g_low, g_high)


def _moe_kernel(r_tile, ea_ref, eb_ref, nv_ref,
                gcur_ref, gnext_ref, sprev_ref, scur_ref, gates_ref, h2_hbm,
                wga_ref, wua_ref, wda_ref, wgb_ref, wub_ref, wdb_ref,
                y_hbm, xbuf, ybuf, gsem, ssem):
    t = pl.program_id(0)
    nvalid = nv_ref[0]
    slot = t % 2

    def gather_start(idx_ref, s):
        for r in range(r_tile):
            pltpu.make_async_copy(h2_hbm.at[idx_ref[0, 0, r]], xbuf.at[s, pl.ds(N_CHUNKS * r, N_CHUNKS)],
                                  gsem.at[s]).start()

    def gather_wait(s):
        pltpu.make_async_copy(xbuf.at[s], xbuf.at[s], gsem.at[s]).wait()

    def scatter_start(idx_ref, s):
        for r in range(r_tile):
            pltpu.make_async_copy(ybuf.at[s, pl.ds(N_CHUNKS * r, N_CHUNKS)], y_hbm.at[idx_ref[0, 0, r]],
                                  ssem).start()

    def scatter_wait():
        pltpu.make_async_copy(ybuf.at[0], ybuf.at[0], ssem).wait()

    @pl.when(t == 0)
    def _():
        ybuf[1] = jnp.zeros(ybuf.shape[1:], F32)
        gather_start(gcur_ref, 0)

    @pl.when(t < nvalid)
    def _():
        @pl.when(t > 0)
        def _():
            scatter_wait()

        gather_wait(slot)
        gather_start(gnext_ref, 1 - slot)
        scatter_start(sprev_ref, 1 - slot)
        x = _slab_to_nat(xbuf.at[slot], r_tile).astype(BF16)
        gts = gates_ref[...]
        y = None
        for (wg, wu, wd, col) in ((wga_ref, wua_ref, wda_ref, 0), (wgb_ref, wub_ref, wdb_ref, 1)):
            g = jnp.dot(x, wg[0, 0].astype(BF16), preferred_element_type=F32)
            u = jnp.dot(x, wu[0, 0].astype(BF16), preferred_element_type=F32)
            hmid = (g * (1.0 / (1.0 + jnp.exp(-g))) * u).astype(BF16)
            ye = gts[:, col:col + 1] * jnp.dot(hmid, wd[0, 0].astype(BF16), preferred_element_type=F32)
            y = ye if y is None else y + ye
        _nat_to_slab(ybuf.at[slot], y, r_tile)

        @pl.when(t == nvalid - 1)
        def _():
            gather_wait(1 - slot)
            scatter_wait()
            scatter_start(scur_ref, slot)
            scatter_wait()


def _moe(h2_slab, rt, wg, wu, wd, layer, n_tok):
    r = R_MOE
    nt = n_tok // r + N_BUCKETS
    bucket = rt[0].astype(jnp.int32)
    n_pad = nt * r - n_tok
    buckets = jnp.arange(N_BUCKETS, dtype=jnp.int32)
    counts = jnp.sum((bucket[:, None] == buckets[None, :]).astype(jnp.int32), axis=0)
    tiles_b = (counts + r - 1) // r
    tile_end = jnp.cumsum(tiles_b)
    nvalid = tile_end[-1]
    pad_end = jnp.cumsum(tiles_b * r - counts)
    pad_bucket = jnp.sum((jnp.arange(n_pad, dtype=jnp.int32)[:, None] >= pad_end[None, :]).astype(jnp.int32),
                         axis=1)
    keys = jnp.concatenate([2 * bucket, 2 * pad_bucket + 1])
    zi = jnp.zeros((n_pad,), jnp.int32)
    zf = jnp.zeros((n_pad,), F32)
    _, gidx, valid, gate_a, gate_b = lax.sort(
        (keys, jnp.concatenate([jnp.arange(n_tok, dtype=jnp.int32), zi]),
         jnp.concatenate([jnp.ones((n_tok,), jnp.int32), zi]),
         jnp.concatenate([rt[1], zf]), jnp.concatenate([rt[2], zf])),
        num_keys=1, is_stable=True)
    rows = jnp.arange(r, dtype=jnp.int32)[None, :]
    spare = jnp.broadcast_to(n_tok + rows, (nt, r))
    sidx = jnp.concatenate([spare[:1], jnp.where(valid.reshape(nt, r) == 1, gidx.reshape(nt, r), spare)], axis=0)
    gates = jnp.stack([gate_a, gate_b], axis=-1)
    tix = jnp.minimum(jnp.arange(nt, dtype=jnp.int32), nvalid - 1)
    tb = jnp.minimum(jnp.sum((tix[:, None] >= tile_end[None, :]).astype(jnp.int32), axis=1), N_BUCKETS - 1)
    grp = tb // N_PAIRS
    ea = (grp * EXPERTS_PER_GROUP + jnp.asarray(PAIR_A, jnp.int32)[tb % N_PAIRS]).astype(jnp.int32)
    eb = (grp * EXPERTS_PER_GROUP + jnp.asarray(PAIR_B, jnp.int32)[tb % N_PAIRS]).astype(jnp.int32)
    nv = nvalid.reshape(1).astype(jnp.int32)
    gidx3 = gidx.reshape(nt, 1, r)
    sidx3 = sidx.reshape(nt + 1, 1, r)

    def smem(index):
        return pl.BlockSpec((1, 1, r), lambda t, ea, eb, nv: (index(t), 0, 0), memory_space=pltpu.SMEM)

    w_in_a = pl.BlockSpec((1, 1, D_MODEL, D_EXPERT), lambda t, ea, eb, nv: (layer, ea[t], 0, 0))
    w_out_a = pl.BlockSpec((1, 1, D_EXPERT, D_MODEL), lambda t, ea, eb, nv: (layer, ea[t], 0, 0))
    w_in_b = pl.BlockSpec((1, 1, D_MODEL, D_EXPERT), lambda t, ea, eb, nv: (layer, eb[t], 0, 0))
    w_out_b = pl.BlockSpec((1, 1, D_EXPERT, D_MODEL), lambda t, ea, eb, nv: (layer, eb[t], 0, 0))
    grid_spec = pltpu.PrefetchScalarGridSpec(
        num_scalar_prefetch=3,
        grid=(nt,),
        in_specs=[smem(lambda t: t), smem(lambda t: jnp.minimum(t + 1, nt - 1)),
                  smem(lambda t: t), smem(lambda t: t + 1),
                  pl.BlockSpec((r, 2), lambda t, ea, eb, nv: (t, 0)),
                  pl.BlockSpec(memory_space=pl.ANY),
                  w_in_a, w_in_a, w_out_a, w_in_b, w_in_b, w_out_b],
        out_specs=pl.BlockSpec(memory_space=pl.ANY),
        scratch_shapes=[pltpu.VMEM((2, r * N_CHUNKS, LANES), F32),
                        pltpu.VMEM((2, r * N_CHUNKS, LANES), F32),
                        pltpu.SemaphoreType.DMA((2,)),
                        pltpu.SemaphoreType.DMA],
    )
    y = pl.pallas_call(
        functools.partial(_moe_kernel, r),
        grid_spec=grid_spec,
        out_shape=jax.ShapeDtypeStruct((n_tok + r, N_CHUNKS, LANES), F32),
        compiler_params=_cparams(1),
        name="moe_experts",
    )(ea, eb, nv, gidx3, gidx3, sidx3, sidx3, gates, h2_slab, wg, wu, wd, wg, wu, wd)
    return y


def _final_kernel(npt, x1_ref, y_ref, g2_ref, gf_ref, op_ref, os_ref):
    i = pl.program_id(0)
    x = x1_ref[...] + g2_ref[0] * _slab_to_nat(y_ref, TM)
    ms = jnp.mean(x * x, axis=-1, keepdims=True)
    out = x * lax.rsqrt(ms + EPS) * gf_ref[...]

    @pl.when(i < npt)
    def _():
        op_ref[...] = out

    @pl.when(i >= npt)
    def _():
        os_ref[...] = out


def _rope_tables(dseq):
    n_freq = HEAD_DIM // 4
    inv = ROPE_THETA ** (-jnp.arange(0, 2 * n_freq, 2, dtype=F32) / (2 * n_freq))
    pos = jnp.arange(dseq)
    row = (pos // GRID_W).astype(F32)
    col = (pos % GRID_W).astype(F32)
    ang_r = row[:, None] * inv
    ang_c = col[:, None] * inv
    ang = jnp.concatenate([ang_r, ang_r, ang_c, ang_c], axis=-1)
    sign = jnp.tile(jnp.concatenate([-jnp.ones((n_freq,), F32), jnp.ones((n_freq,), F32)]), 2)
    cos = jnp.tile(jnp.cos(ang), (1, 2))
    sin = jnp.tile(jnp.sin(ang) * sign, (1, 2))
    cos = jnp.concatenate([jnp.ones((TM, LANES), F32), cos], axis=0)
    sin = jnp.concatenate([jnp.zeros((TM, LANES), F32), sin], axis=0)
    return cos, sin


def _dup_halves(w, n_heads):
    lead = w.shape[:-1]
    w = w.reshape(lead + (n_heads, 1, HEAD_DIM))
    w = jnp.broadcast_to(w, lead + (n_heads, 2, HEAD_DIM))
    return w.reshape(lead + (n_heads * 2 * HEAD_DIM,))


def kernel(x_prompt, x_sample, cache_k_a, cache_v_a, cache_k_b, cache_v_b, c, c_ctx, w_mod, b_mod, g_attn, g_ffn, w_qkv_a, w_o_a, g_q_a, g_k_a, w_qkv_b, w_o_b, lam_q1, lam_k1, lam_q2, lam_k2, g_sub_b, w_router, b_router, w_gate, w_up, w_down, g_final):
    nbp, seq, d = x_prompt.shape
    nbs, dseq, _ = x_sample.shape
    depth = w_mod.shape[0]
    assert d == D_MODEL and seq % TM == 0 and dseq % TM == 0
    tp, ts = nbp * seq, nbs * dseq
    t_all = tp + ts
    npt = tp // TM
    tpb = dseq // TM
    nti = t_all // TM
    assert nbs + 1 <= N_MOD_ROWS

    xp = x_prompt.reshape(tp, d)
    xs = x_sample.reshape(ts, d)

    cvec = jnp.zeros((N_MOD_ROWS, d), F32).at[0].set(c_ctx).at[1:1 + nbs].set(c)
    mod = _modulation(cvec, w_mod, b_mod).reshape(depth * N_MOD_ROWS * N_MOD, 1, d)

    cos_t, sin_t = _rope_tables(dseq)
    qscale = HEAD_DIM ** -0.5 * math.log2(math.e)

    nqk = D_MODEL + 2 * A_KV_HEADS * HEAD_DIM
    lane_id = jnp.arange(nqk)
    seg = jnp.where(lane_id < D_MODEL, lane_id // HEAD_DIM, A_HEADS + (lane_id - D_MODEL) // (2 * HEAD_DIM))
    seg_w = jnp.where(lane_id < D_MODEL, 1.0 / HEAD_DIM, 1.0 / (2 * HEAD_DIM))
    onehot = (seg[:, None] == jnp.arange(LANES)[None, :]).astype(F32)
    eq = (onehot * seg_w[:, None]).astype(BF16)
    et = jnp.concatenate([onehot.T, onehot.T], axis=0).astype(BF16)

    w_router_p = jnp.zeros((d, LANES), F32).at[:, :N_EXPERTS].set(w_router)
    wr_hi = w_router_p.astype(BF16)
    wr_lo = (w_router_p - wr_hi.astype(F32)).astype(BF16)
    br = jnp.zeros((1, LANES), F32).at[0, :N_EXPERTS].set(b_router)

    past = cache_k_a.shape[2]
    ck_a = _dup_halves(cache_k_a.reshape(nbs, -1, past, A_KV_HEADS * HEAD_DIM), A_KV_HEADS).astype(BF16)
    cv_a = _dup_halves(cache_v_a.reshape(nbs, -1, past, A_KV_HEADS * HEAD_DIM), A_KV_HEADS).astype(BF16)
    ck_b = cache_k_b.reshape(nbs, -1, past, D_MODEL).astype(BF16)
    cv_b = cache_v_b.reshape(nbs, -1, past, D_MODEL).astype(BF16)

    tok_spec = pl.BlockSpec((TM, D_MODEL), lambda i: (i, 0))
    tiles_per_seq = seq // TM

    def state_spec(w, j):
        def index(i):
            ii = jnp.minimum(i, npt - 1)
            return (ii // tiles_per_seq, j, ii % tiles_per_seq, 0)
        return pl.BlockSpec((1, 1, TM, w), index)

    any_spec = pl.BlockSpec(memory_space=pl.ANY)
    n_a, n_b = (depth + 1) // 2, depth // 2
    wkv = 2 * A_KV_HEADS * HEAD_DIM
    state_k_a = jnp.zeros((nbp, n_a, seq, wkv // 2), F32)
    state_v_a = jnp.zeros((nbp, n_a, seq, wkv // 2), F32)
    state_k_b = jnp.zeros((nbp, n_b, seq, d), F32)
    state_v_b = jnp.zeros((nbp, n_b, seq, d), F32)
    prev = None
    for l in range(depth):
        j = l // 2
        has_prev = prev is not None
        if has_prev:
            x1_prev, y_prev, l_prev = prev
            x_args = [x1_prev, y_prev, mod]
            x_specs = _x_specs(True, npt)
            x_specs[2] = _mod_spec(l_prev, 5, npt, tpb)
        else:
            x_args = [xp, xs]
            x_specs = _x_specs(False, npt)
        mod_args = lambda ks: [mod] * len(ks)
        mod_specs = lambda ks: [_mod_spec(l, k, npt, tpb) for k in ks]
        g_attn_l = g_attn[l].reshape(1, d)

        if l % 2 == 0:
            wq, wk, wv = jnp.split(w_qkv_a[j], [A_HEADS * HEAD_DIM, (A_HEADS + A_KV_HEADS) * HEAD_DIM], axis=-1)
            w_all = jnp.concatenate([wq, _dup_halves(wk, A_KV_HEADS), _dup_halves(wv, A_KV_HEADS)], axis=-1).astype(BF16)
            gvec = jnp.concatenate([jnp.tile(g_q_a[j], A_HEADS) * qscale,
                                    jnp.tile(g_k_a[j], 2 * A_KV_HEADS)]).reshape(1, nqk)
            n_in = len(x_specs) + 11
            q, k, v, state_k_a, state_v_a = pl.pallas_call(
                functools.partial(_proj_a_kernel, has_prev, npt),
                grid=(nti,),
                in_specs=x_specs + mod_specs((0, 1)) + [
                    _const_spec((1, d)), _const_spec(w_all.shape), _const_spec(eq.shape),
                    _const_spec(et.shape), _const_spec((1, nqk)), _rope_spec(npt, tpb), _rope_spec(npt, tpb),
                    any_spec, any_spec],
                out_specs=[tok_spec, pl.BlockSpec((TM, wkv), lambda i: (i, 0)),
                           pl.BlockSpec((TM, wkv), lambda i: (i, 0)),
                           state_spec(wkv // 2, j), state_spec(wkv // 2, j)],
                out_shape=[jax.ShapeDtypeStruct((t_all, d), BF16),
                           jax.ShapeDtypeStruct((t_all, wkv), BF16),
                           jax.ShapeDtypeStruct((t_all, wkv), BF16),
                           jax.ShapeDtypeStruct(state_k_a.shape, F32),
                           jax.ShapeDtypeStruct(state_v_a.shape, F32)],
                input_output_aliases={n_in - 2: 3, n_in - 1: 4},
                compiler_params=_cparams(1),
                name="proj_a",
            )(*x_args, *mod_args((0, 1)), g_attn_l, w_all, eq, et, gvec, cos_t, sin_t, state_k_a, state_v_a)
            o_p, o_s = _attention("a", TQ_A, q, k, v, nbp, seq, nbs, dseq, ck_a, cv_a, j)
            w_o = w_o_a[j].astype(BF16)
        else:
            lam_init = 0.8 - 0.6 * math.exp(-0.3 * l)
            lam = (jnp.exp(jnp.sum((lam_q1[j] * lam_k1[j]).astype(F32)))
                   - jnp.exp(jnp.sum((lam_q2[j] * lam_k2[j]).astype(F32))) + lam_init).reshape(1, 1)
            wq, wk, wv = jnp.split(w_qkv_b[j], 3, axis=-1)
            w_all = jnp.concatenate([wq * qscale, wk, wv], axis=-1).astype(BF16)
            n_in = len(x_specs) + 8
            q, k, v, state_k_b, state_v_b = pl.pallas_call(
                functools.partial(_proj_b_kernel, has_prev, npt),
                grid=(nti,),
                in_specs=x_specs + mod_specs((0, 1)) + [
                    _const_spec((1, d)), _const_spec(w_all.shape), _rope_spec(npt, tpb), _rope_spec(npt, tpb),
                    any_spec, any_spec],
                out_specs=[tok_spec, tok_spec, tok_spec, state_spec(d, j), state_spec(d, j)],
                out_shape=[jax.ShapeDtypeStruct((t_all, d), BF16),
                           jax.ShapeDtypeStruct((t_all, d), BF16),
                           jax.ShapeDtypeStruct((t_all, d), BF16),
                           jax.ShapeDtypeStruct(state_k_b.shape, F32),
                           jax.ShapeDtypeStruct(state_v_b.shape, F32)],
                input_output_aliases={n_in - 2: 3, n_in - 1: 4},
                compiler_params=_cparams(1),
                name="proj_b",
            )(*x_args, *mod_args((0, 1)), g_attn_l, w_all, cos_t, sin_t, state_k_b, state_v_b)
            gsub = (g_sub_b[j] * (1.0 - lam_init)).reshape(1, 2 * HEAD_DIM)
            o_p, o_s = _attention("b", TQ_B, q, k, v, nbp, seq, nbs, dseq, ck_b, cv_b, j, lam=lam, gsub=gsub)
            w_o = w_o_b[j].astype(BF16)

        x1, h2, rt = pl.pallas_call(
            functools.partial(_kb_kernel, has_prev, npt),
            grid=(nti,),
            in_specs=x_specs + [
                pl.BlockSpec((TM, d), lambda i: (jnp.minimum(i, npt - 1), 0)),
                pl.BlockSpec((TM, d), lambda i: (jnp.maximum(i - npt, 0), 0))] + mod_specs((2, 3, 4)) + [
                _const_spec((d, d)), _const_spec((1, d)), _const_spec((d, LANES)), _const_spec((d, LANES)),
                _const_spec((1, LANES))],
            out_specs=[tok_spec, pl.BlockSpec((TM * N_CHUNKS, LANES), lambda i: (i, 0)),
                       pl.BlockSpec((8, TM), lambda i: (0, i))],
            out_shape=[jax.ShapeDtypeStruct((t_all, d), F32),
                       jax.ShapeDtypeStruct((t_all * N_CHUNKS, LANES), F32),
                       jax.ShapeDtypeStruct((8, t_all), F32)],
            scratch_shapes=[pltpu.VMEM((LANES, TM), F32)],
            compiler_params=_cparams(1),
            name="oproj_router",
        )(*x_args, o_p, o_s, *mod_args((2, 3, 4)), w_o, g_ffn[l].reshape(1, d), wr_hi, wr_lo, br)

        y = _moe(h2.reshape(t_all, N_CHUNKS, LANES), rt, w_gate, w_up, w_down, l, t_all)
        prev = (x1, y.reshape((t_all + R_MOE) * N_CHUNKS, LANES), l)

    x1_prev, y_prev, l_prev = prev
    y_p, y_s = pl.pallas_call(
        functools.partial(_final_kernel, npt),
        grid=(nti,),
        in_specs=[tok_spec, pl.BlockSpec((TM * N_CHUNKS, LANES), lambda i: (i, 0)),
                  _mod_spec(l_prev, 5, npt, tpb), _const_spec((1, d))],
        out_specs=[pl.BlockSpec((TM, d), lambda i: (jnp.minimum(i, npt - 1), 0)),
                   pl.BlockSpec((TM, d), lambda i: (jnp.maximum(i - npt, 0), 0))],
        out_shape=[jax.ShapeDtypeStruct((tp, d), F32), jax.ShapeDtypeStruct((ts, d), F32)],
        compiler_params=_cparams(1),
        name="final_norm",
    )(x1_prev, y_prev, mod, g_final.reshape(1, d))

    return (y_p.reshape(nbp, seq, d), y_s.reshape(nbs, dseq, d),
            state_k_a.reshape(nbp, n_a, seq, A_KV_HEADS, HEAD_DIM),
            state_v_a.reshape(nbp, n_a, seq, A_KV_HEADS, HEAD_DIM),
            state_k_b.reshape(nbp, n_b, seq, B_HEADS, 2, HEAD_DIM),
            state_v_b.reshape(nbp, n_b, seq, B_HEADS, 2 * HEAD_DIM))
```

```python
import functools
import math

import jax
import jax.numpy as jnp
from jax import lax
from jax.experimental import pallas as pl
from jax.experimental.pallas import tpu as pltpu

F32 = jnp.float32
BF16 = jnp.bfloat16

D_MODEL = 1024
HEAD_DIM = 64
A_HEADS = 16
A_KV_HEADS = 4
B_HEADS = 8
N_EXPERTS = 16
N_GROUPS = 4
EXPERTS_PER_GROUP = 4
D_EXPERT = 512
N_MOD = 6
N_MOD_ROWS = 16
ROPE_THETA = 10000.0
GRID_W = 64
EPS = 1e-6

LANES = 128
N_CHUNKS = D_MODEL // LANES
TM = 256
TQ_A = 256
TQ_B = 512
R_MOE = 256
COPY_GROUP = 32
N_PAIRS = 6
N_BUCKETS = N_GROUPS * N_PAIRS
PAIR_A = (0, 0, 0, 1, 1, 3)
PAIR_B = (1, 2, 3, 3, 2, 2)
VMEM_LIMIT = 56 * 1024 * 1024


def _cparams(n_axes=1):
    return pltpu.CompilerParams(dimension_semantics=("arbitrary",) * n_axes,
                                vmem_limit_bytes=VMEM_LIMIT)


def _slab_to_nat(ref, tm):
    return jnp.concatenate([ref[pl.ds(c, tm, stride=N_CHUNKS), :] for c in range(N_CHUNKS)], axis=-1)


def _nat_to_slab(ref, val, tm):
    for c in range(N_CHUNKS):
        ref[pl.ds(c, tm, stride=N_CHUNKS), :] = val[:, LANES * c:LANES * (c + 1)]


def _norm_mod(x, g, shift, scale):
    ms = jnp.mean(x * x, axis=-1, keepdims=True)
    return x * lax.rsqrt(ms + EPS) * g * (1.0 + scale) + shift


def _rope_cols(x, cos, sin, second_of_pair):
    outs = []
    for c in range(x.shape[1] // LANES):
        xc = x[:, LANES * c:LANES * (c + 1)]
        partner = jnp.where(second_of_pair, pltpu.roll(xc, 16, 1), pltpu.roll(xc, LANES - 16, 1))
        outs.append(xc * cos + partner * sin)
    return jnp.concatenate(outs, axis=-1)


def _stream_x(i, npt, refs, has_prev, tm):
    if has_prev:
        x1_ref, y_ref, g2_ref = refs
        return x1_ref[...] + g2_ref[0] * _slab_to_nat(y_ref, tm)
    xp_ref, xs_ref = refs
    return jnp.where(i < npt, xp_ref[...], xs_ref[...])


def _mod_kernel(cv_ref, w_ref, b_ref, o_ref):
    cv = cv_ref[...]
    s = cv * (1.0 / (1.0 + jnp.exp(-cv)))
    o_ref[0] = jnp.dot(s.astype(BF16), w_ref[0].astype(BF16), preferred_element_type=F32) + b_ref[0]


def _modulation(cvec, w_mod, b_mod):
    depth, d, n = w_mod.shape
    tn = 1024
    return pl.pallas_call(
        _mod_kernel,
        grid=(depth, n // tn),
        in_specs=[pl.BlockSpec((N_MOD_ROWS, d), lambda l, j: (0, 0)),
                  pl.BlockSpec((1, d, tn), lambda l, j: (l, 0, j)),
                  pl.BlockSpec((1, 1, tn), lambda l, j: (l, 0, j))],
        out_specs=pl.BlockSpec((1, N_MOD_ROWS, tn), lambda l, j: (l, 0, j)),
        out_shape=jax.ShapeDtypeStruct((depth, N_MOD_ROWS, n), F32),
        compiler_params=_cparams(2),
        name="modulation",
    )(cvec, w_mod, b_mod.reshape(depth, 1, n))


def _proj_a_kernel(has_prev, npt, *refs):
    n_x = 3 if has_prev else 2
    xrefs, refs = refs[:n_x], refs[n_x:]
    (sh_ref, sc_ref, ga_ref, w_ref, eq_ref, et_ref, gv_ref, cos_ref, sin_ref, _, _,
     q_ref, k_ref, v_ref, ks_ref, vs_ref) = refs
    i = pl.program_id(0)
    x = _stream_x(i, npt, xrefs, has_prev, TM)
    hb = _norm_mod(x, ga_ref[...], sh_ref[0], sc_ref[0]).astype(BF16)
    qkv = jnp.dot(hb, w_ref[...], preferred_element_type=F32)
    nqk = D_MODEL + 2 * A_KV_HEADS * HEAD_DIM
    qk = qkv[:, :nqk]
    ms = jnp.dot((qk * qk).astype(BF16), eq_ref[...], preferred_element_type=F32)
    inv = lax.rsqrt(ms + EPS)
    inv_hi = inv.astype(BF16)
    inv_lo = (inv - inv_hi.astype(F32)).astype(BF16)
    invf = jnp.dot(jnp.concatenate([inv_hi, inv_lo], axis=-1), et_ref[...], preferred_element_type=F32)
    qkn = qk * invf * gv_ref[...]
    lane = lax.broadcasted_iota(jnp.int32, (1, LANES), 1)
    second = (lane & 16) != 0
    lo = lane < HEAD_DIM
    qkr = _rope_cols(qkn, cos_ref[...], sin_ref[...], second)
    q_ref[...] = qkr[:, :D_MODEL].astype(BF16)
    k_ref[...] = qkr[:, D_MODEL:].astype(BF16)
    vd = qkv[:, nqk:]
    v_ref[...] = vd.astype(BF16)

    @pl.when(i < npt)
    def _():
        kd = qkn[:, D_MODEL:]
        for j in range(2):
            ks_ref[0, 0, :, LANES * j:LANES * (j + 1)] = jnp.where(
                lo, kd[:, LANES * 2 * j:LANES * (2 * j + 1)], kd[:, LANES * (2 * j + 1):LANES * (2 * j + 2)])
            vs_ref[0, 0, :, LANES * j:LANES * (j + 1)] = jnp.where(
                lo, vd[:, LANES * 2 * j:LANES * (2 * j + 1)], vd[:, LANES * (2 * j + 1):LANES * (2 * j + 2)])


def _proj_b_kernel(has_prev, npt, *refs):
    n_x = 3 if has_prev else 2
    xrefs, refs = refs[:n_x], refs[n_x:]
    (sh_ref, sc_ref, ga_ref, w_ref, cos_ref, sin_ref, _, _,
     q_ref, k_ref, v_ref, ks_ref, vs_ref) = refs
    i = pl.program_id(0)
    x = _stream_x(i, npt, xrefs, has_prev, TM)
    hb = _norm_mod(x, ga_ref[...], sh_ref[0], sc_ref[0]).astype(BF16)
    qkv = jnp.dot(hb, w_ref[...], preferred_element_type=F32)
    lane = lax.broadcasted_iota(jnp.int32, (1, LANES), 1)
    second = (lane & 16) != 0
    qk = qkv[:, :2 * D_MODEL]
    qkr = _rope_cols(qk, cos_ref[...], sin_ref[...], second)
    q_ref[...] = qkr[:, :D_MODEL].astype(BF16)
    k_ref[...] = qkr[:, D_MODEL:].astype(BF16)
    v = qkv[:, 2 * D_MODEL:]
    v_ref[...] = v.astype(BF16)

    @pl.when(i < npt)
    def _():
        ks_ref[0, 0] = qk[:, D_MODEL:]
        vs_ref[0, 0] = v


def _x_specs(has_prev, npt):
    if has_prev:
        return [pl.BlockSpec((TM, D_MODEL), lambda i: (i, 0)),
                pl.BlockSpec((TM * N_CHUNKS, LANES), lambda i: (i, 0)),
                None]
    return [pl.BlockSpec((TM, D_MODEL), lambda i: (jnp.minimum(i, npt - 1), 0)),
            pl.BlockSpec((TM, D_MODEL), lambda i: (jnp.maximum(i - npt, 0), 0))]


def _mod_spec(layer, k, npt, tpb):
    def index(i):
        row = jnp.where(i < npt, 0, 1 + (i - npt) // tpb)
        return ((layer * N_MOD_ROWS + row) * N_MOD + k, 0, 0)
    return pl.BlockSpec((1, 1, D_MODEL), index)


def _rope_spec(npt, tpb):
    return pl.BlockSpec((TM, LANES), lambda i: (jnp.where(i < npt, 0, 1 + (i - npt) % tpb), 0))


def _const_spec(shape):
    nd = len(shape)
    return pl.BlockSpec(shape, lambda i: (0,) * nd)


def _lane_masks():
    lane = lax.broadcasted_iota(jnp.int32, (1, LANES), 1)
    lo = lane < HEAD_DIM
    mlo = jnp.where(lo, 1.0, 0.0).astype(BF16)
    mhi = jnp.where(lo, 0.0, 1.0).astype(BF16)
    return lo, mlo, mhi


_NT_DIMS = (((1,), (1,)), ((), ()))


def _scores_stage(lhs, cols, lk, has_cache, k_ref, ck_ref, s_scr, m_scr):
    s = lax.dot_general(lhs, k_ref[:, cols], _NT_DIMS, preferred_element_type=F32)
    s_scr[:, 0:lk] = s
    mp = s[:, 0:LANES]
    for c in range(1, lk // LANES):
        mp = jnp.maximum(mp, s[:, LANES * c:LANES * (c + 1)])
    if has_cache:
        s2 = lax.dot_general(lhs, ck_ref[0, 0, :, cols], _NT_DIMS, preferred_element_type=F32)
        past = s2.shape[1]
        s_scr[:, lk:lk + past] = s2
        for c in range(past // LANES):
            mp = jnp.maximum(mp, s2[:, LANES * c:LANES * (c + 1)])
    m_scr[...] = mp


def _weighted_values(s_scr, m_scr, lk, has_cache, v_ref, cv_ref, cols):
    m = jnp.max(m_scr[...], axis=-1, keepdims=True)
    pb = jnp.exp2(s_scr[...] - m).astype(BF16)
    rhs = jnp.concatenate([v_ref[:, cols], jnp.ones((lk, LANES), BF16)], axis=-1)
    res = jnp.dot(pb[:, 0:lk], rhs, preferred_element_type=F32)
    if has_cache:
        cv = cv_ref[0, 0, :, cols]
        rhs2 = jnp.concatenate([cv, jnp.ones(cv.shape, BF16)], axis=-1)
        res = res + jnp.dot(pb[:, lk:], rhs2, preferred_element_type=F32)
    return res


def _attn_a_kernel(tq, has_cache, *refs):
    if has_cache:
        q_ref, k_ref, v_ref, ck_ref, cv_ref, o_ref, s0, s1, m0, m1 = refs
    else:
        q_ref, k_ref, v_ref, o_ref, s0, s1, m0, m1 = refs
        ck_ref = cv_ref = None
    lk = k_ref.shape[0]
    lo, mlo, mhi = _lane_masks()
    bufs = ((s0, m0), (s1, m1))

    def scores(g, s_scr, m_scr):
        qa = q_ref[:, LANES * 2 * g:LANES * (2 * g + 1)]
        qb = q_ref[:, LANES * (2 * g + 1):LANES * (2 * g + 2)]
        lhs = jnp.concatenate([qa * mlo, qa * mhi, qb * mlo, qb * mhi], axis=0)
        _scores_stage(lhs, slice(LANES * g, LANES * (g + 1)), lk, has_cache, k_ref, ck_ref, s_scr, m_scr)

    def finish(g, s_scr, m_scr):
        cols = slice(LANES * g, LANES * (g + 1))
        res = _weighted_values(s_scr, m_scr, lk, has_cache, v_ref, cv_ref, cols)
        o = res[:, 0:LANES] * (1.0 / res[:, LANES:])
        o_ref[:, LANES * 2 * g:LANES * (2 * g + 1)] = jnp.where(lo, o[0:tq], o[tq:2 * tq]).astype(BF16)
        o_ref[:, LANES * (2 * g + 1):LANES * (2 * g + 2)] = jnp.where(
            lo, o[2 * tq:3 * tq], o[3 * tq:4 * tq]).astype(BF16)

    scores(0, *bufs[0])
    for g in range(A_KV_HEADS):
        if g + 1 < A_KV_HEADS:
            scores(g + 1, *bufs[(g + 1) % 2])
        finish(g, *bufs[g % 2])


def _attn_b_kernel(tq, has_cache, *refs):
    if has_cache:
        lam_ref, q_ref, k_ref, v_ref, ck_ref, cv_ref, gs_ref, o_ref, s0, s1, m0, m1 = refs
    else:
        lam_ref, q_ref, k_ref, v_ref, gs_ref, o_ref, s0, s1, m0, m1 = refs
        ck_ref = cv_ref = None
    lk = k_ref.shape[0]
    lam = lam_ref[0, 0]
    _, mlo, mhi = _lane_masks()
    bufs = ((s0, m0), (s1, m1))

    def scores(h, s_scr, m_scr):
        cols = slice(LANES * h, LANES * (h + 1))
        qc = q_ref[:, cols]
        lhs = jnp.concatenate([qc * mlo, qc * mhi], axis=0)
        _scores_stage(lhs, cols, lk, has_cache, k_ref, ck_ref, s_scr, m_scr)

    def finish(h, s_scr, m_scr):
        cols = slice(LANES * h, LANES * (h + 1))
        res = _weighted_values(s_scr, m_scr, lk, has_cache, v_ref, cv_ref, cols)
        on = res[:, 0:LANES] * (1.0 / res[:, LANES:])
        o = on[0:tq] - lam * on[tq:2 * tq]
        ms = jnp.mean(o * o, axis=-1, keepdims=True)
        o_ref[:, cols] = (o * lax.rsqrt(ms + EPS) * gs_ref[...]).astype(BF16)

    scores(0, *bufs[0])
    for h in range(B_HEADS):
        if h + 1 < B_HEADS:
            scores(h + 1, *bufs[(h + 1) % 2])
        finish(h, *bufs[h % 2])


def _attention(kind, tq, q, k, v, nbp, seq, nbs, dseq, cache_k, cache_v, layer_j, lam=None, gsub=None):
    t_all = q.shape[0]
    wk = k.shape[1]
    kern = _attn_a_kernel if kind == "a" else _attn_b_kernel
    pre_args, pre_specs_1, pre_specs_2 = [], [], []
    post_args, post_specs_1, post_specs_2 = [], [], []
    if kind == "b":
        pre_args = [lam]
        pre_specs_1 = [pl.BlockSpec(memory_space=pltpu.SMEM)]
        pre_specs_2 = [pl.BlockSpec(memory_space=pltpu.SMEM)]
        post_args = [gsub]
        post_specs_1 = [pl.BlockSpec((1, LANES), lambda b: (0, 0))]
        post_specs_2 = [pl.BlockSpec((1, LANES), lambda b, j: (0, 0))]

    stack = 4 if kind == "a" else 2

    def scratch(rows, keys):
        return [pltpu.VMEM((rows, keys), F32), pltpu.VMEM((rows, keys), F32),
                pltpu.VMEM((rows, LANES), F32), pltpu.VMEM((rows, LANES), F32)]

    tqp = min(tq, seq)
    nqp = seq // tqp
    out_p = pl.pallas_call(
        functools.partial(kern, tqp, False),
        grid=(nbp * nqp,),
        in_specs=pre_specs_1 + [
            pl.BlockSpec((tqp, D_MODEL), lambda b: (b, 0)),
            pl.BlockSpec((seq, wk), lambda b: (b // nqp, 0)),
            pl.BlockSpec((seq, wk), lambda b: (b // nqp, 0))] + post_specs_1,
        out_specs=pl.BlockSpec((tqp, D_MODEL), lambda b: (b, 0)),
        out_shape=jax.ShapeDtypeStruct((nbp * seq, D_MODEL), BF16),
        scratch_shapes=scratch(stack * tqp, seq),
        compiler_params=_cparams(1),
        name="attn_%s_ctx" % kind,
    )(*pre_args, q, k, v, *post_args)

    nq = dseq // tq
    qoff = (nbp * seq) // tq
    koff = (nbp * seq) // dseq
    assert (nbp * seq) % dseq == 0
    past = cache_k.shape[2]
    out_s = pl.pallas_call(
        functools.partial(kern, tq, True),
        grid=(nbs, nq),
        in_specs=pre_specs_2 + [
            pl.BlockSpec((tq, D_MODEL), lambda b, j: (qoff + b * nq + j, 0)),
            pl.BlockSpec((dseq, wk), lambda b, j: (koff + b, 0)),
            pl.BlockSpec((dseq, wk), lambda b, j: (koff + b, 0)),
            pl.BlockSpec((1, 1, past, wk), lambda b, j: (b, layer_j, 0, 0)),
            pl.BlockSpec((1, 1, past, wk), lambda b, j: (b, layer_j, 0, 0))] + post_specs_2,
        out_specs=pl.BlockSpec((tq, D_MODEL), lambda b, j: (b * nq + j, 0)),
        out_shape=jax.ShapeDtypeStruct((nbs * dseq, D_MODEL), BF16),
        scratch_shapes=scratch(stack * tq, dseq + past),
        compiler_params=_cparams(2),
        name="attn_%s_lat" % kind,
    )(*pre_args, q, k, v, cache_k, cache_v, *post_args)
    del t_all
    return out_p, out_s


def _first_max_index(vals, m):
    idx = jnp.full_like(m, float(len(vals) - 1))
    for j in range(len(vals) - 2, -1, -1):
        idx = jnp.where(vals[j] == m, float(j), idx)
    return idx


def _select(idx, vals):
    out = vals[-1]
    for j in range(len(vals) - 2, -1, -1):
        out = jnp.where(idx == float(j), vals[j], out)
    return out


def _kb_kernel(has_prev, npt, *refs):
    n_x = 3 if has_prev else 2
    xrefs, refs = refs[:n_x], refs[n_x:]
    (op_ref, os_ref, g1_ref, sh_ref, sc_ref, wo_ref, gf_ref, wrh_ref, wrl_ref, br_ref,
     x1_ref, h2_ref, rt_ref, lt_scr) = refs
    i = pl.program_id(0)
    x = _stream_x(i, npt, xrefs, has_prev, TM)
    o = jnp.where(i < npt, op_ref[...], os_ref[...])
    a = jnp.dot(o, wo_ref[...], preferred_element_type=F32)
    x1 = x + g1_ref[0] * a
    x1_ref[...] = x1
    h2 = _norm_mod(x1, gf_ref[...], sh_ref[0], sc_ref[0])
    _nat_to_slab(h2_ref, h2, TM)
    h_hi = h2.astype(BF16)
    h_lo = (h2 - h_hi.astype(F32)).astype(BF16)
    logits = (jnp.dot(h_hi, wrh_ref[...], preferred_element_type=F32)
              + jnp.dot(h_hi, wrl_ref[...], preferred_element_type=F32)
              + jnp.dot(h_lo, wrh_ref[...], preferred_element_type=F32)
              + br_ref[...])
    lt_scr[...] = logits.T
    sc = []
    for e in range(N_EXPERTS):
        z = lt_scr[e:e + 1, :]
        sc.append(1.0 / (1.0 + jnp.exp(-z)))
    ninf = jnp.float32(-jnp.inf)
    gscore, gtop = [], []
    for g in range(N_GROUPS):
        a4 = sc[EXPERTS_PER_GROUP * g:EXPERTS_PER_GROUP * (g + 1)]
        m1 = jnp.maximum(jnp.maximum(a4[0], a4[1]), jnp.maximum(a4[2], a4[3]))
        i1 = _first_max_index(a4, m1)
        b4 = [jnp.where(i1 == float(j), ninf, a4[j]) for j in range(EXPERTS_PER_GROUP)]
        m2 = jnp.maximum(jnp.maximum(b4[0], b4[1]), jnp.maximum(b4[2], b4[3]))
        i2 = _first_max_index(b4, m2)
        gscore.append(m1 + m2)
        gtop.append((m1, i1, m2, i2))
    gm = jnp.maximum(jnp.maximum(gscore[0], gscore[1]), jnp.maximum(gscore[2], gscore[3]))
    best = _first_max_index(gscore, gm)
    m1 = _select(best, [t[0] for t in gtop])
    i1 = _select(best, [t[1] for t in gtop])
    m2 = _select(best, [t[2] for t in gtop])
    i2 = _select(best, [t[3] for t in gtop])
    den = m1 + m2
    first_low = i1 < i2
    ia = jnp.minimum(i1, i2)
    ib = jnp.maximum(i1, i2)
    pid = jnp.where(ia == 0.0, ib - 1.0, jnp.where(ia == 1.0, 6.0 - ib, 5.0))
    g_low = jnp.where(first_low, m1, m2) / den
    g_high = jnp.where(first_low, m2, m1) / den
    swapped = pid == 5.0
    rt_ref[...] = jnp.zeros(rt_ref.shape, F32)
    rt_ref[0:1, :] = best * float(N_PAIRS) + pid
    rt_ref[1:2, :] = jnp.where(swapped, g_high, g_low)
    rt_ref[2:3, :] = jnp.where(swapped, g_low, g_high)


def _moe_kernel(r_tile, ea_ref, eb_ref, nv_ref,
                gcur_ref, gnext_ref, sprev_ref, scur_ref, gates_ref, h2_hbm,
                wga_ref, wua_ref, wda_ref, wgb_ref, wub_ref, wdb_ref,
                y_hbm, xbuf, ybuf, gsem, ssem):
    t = pl.program_id(0)
    nt = pl.num_programs(0)
    nvalid = nv_ref[0]
    slot = t % 2
    n_cur = nv_ref[1 + t]
    n_next = nv_ref[1 + jnp.minimum(t + 1, nt - 1)]
    n_scatter = jnp.where(t == 0, r_tile, nv_ref[jnp.maximum(t, 1)])
    n_scatter_before = jnp.where(t <= 1, r_tile, nv_ref[jnp.maximum(t - 1, 1)])

    def for_groups(n, fn):
        for c in range(r_tile // COPY_GROUP):
            pl.when(c * COPY_GROUP < n)(functools.partial(fn, c))

    def group_rows(buf, s, c):
        return buf.at[s, pl.ds(N_CHUNKS * COPY_GROUP * c, N_CHUNKS * COPY_GROUP)]

    def gather_start(idx_ref, s, n):
        def go(c):
            for r in range(COPY_GROUP * c, COPY_GROUP * (c + 1)):
                pltpu.make_async_copy(h2_hbm.at[idx_ref[0, 0, r]], xbuf.at[s, pl.ds(N_CHUNKS * r, N_CHUNKS)],
                                      gsem.at[s]).start()
        for_groups(n, go)

    def gather_wait(s, n):
        for_groups(n, lambda c: pltpu.make_async_copy(group_rows(xbuf, s, c), group_rows(xbuf, s, c),
                                                      gsem.at[s]).wait())

    def scatter_start(idx_ref, s, n):
        def go(c):
            for r in range(COPY_GROUP * c, COPY_GROUP * (c + 1)):
                pltpu.make_async_copy(ybuf.at[s, pl.ds(N_CHUNKS * r, N_CHUNKS)], y_hbm.at[idx_ref[0, 0, r]],
                                      ssem).start()
        for_groups(n, go)

    def scatter_wait(n):
        for_groups(n, lambda c: pltpu.make_async_copy(group_rows(ybuf, 0, c), group_rows(ybuf, 0, c),
                                                      ssem).wait())

    @pl.when(t == 0)
    def _():
        ybuf[1] = jnp.zeros(ybuf.shape[1:], F32)
        xbuf[...] = jnp.zeros(xbuf.shape, F32)
        gather_start(gcur_ref, 0, n_cur)

    @pl.when(t < nvalid)
    def _():
        @pl.when(t > 0)
        def _():
            scatter_wait(n_scatter_before)

        gather_wait(slot, n_cur)
        gather_start(gnext_ref, 1 - slot, n_next)
        scatter_start(sprev_ref, 1 - slot, n_scatter)
        x = _slab_to_nat(xbuf.at[slot], r_tile).astype(BF16)
        gts = gates_ref[...]
        y = None
        for (wg, wu, wd, col) in ((wga_ref, wua_ref, wda_ref, 0), (wgb_ref, wub_ref, wdb_ref, 1)):
            g = jnp.dot(x, wg[0, 0].astype(BF16), preferred_element_type=F32)
            u = jnp.dot(x, wu[0, 0].astype(BF16), preferred_element_type=F32)
            hmid = (g * (1.0 / (1.0 + jnp.exp(-g))) * u).astype(BF16)
            ye = gts[:, col:col + 1] * jnp.dot(hmid, wd[0, 0].astype(BF16), preferred_element_type=F32)
            y = ye if y is None else y + ye
        _nat_to_slab(ybuf.at[slot], y, r_tile)

        @pl.when(t == nvalid - 1)
        def _():
            gather_wait(1 - slot, n_next)
            scatter_wait(n_scatter)
            scatter_start(scur_ref, slot, n_cur)
            scatter_wait(n_cur)


def _moe(h2_slab, rt, wg, wu, wd, layer, n_tok):
    r = R_MOE
    nt = n_tok // r + N_BUCKETS
    bucket = rt[0].astype(jnp.int32)
    n_pad = nt * r - n_tok
    buckets = jnp.arange(N_BUCKETS, dtype=jnp.int32)
    counts = jnp.sum((bucket[:, None] == buckets[None, :]).astype(jnp.int32), axis=0)
    tiles_b = (counts + r - 1) // r
    tile_end = jnp.cumsum(tiles_b)
    nvalid = tile_end[-1]
    pad_end = jnp.cumsum(tiles_b * r - counts)
    pad_bucket = jnp.sum((jnp.arange(n_pad, dtype=jnp.int32)[:, None] >= pad_end[None, :]).astype(jnp.int32),
                         axis=1)
    keys = jnp.concatenate([2 * bucket, 2 * pad_bucket + 1])
    zi = jnp.zeros((n_pad,), jnp.int32)
    zf = jnp.zeros((n_pad,), F32)
    _, gidx, valid, gate_a, gate_b = lax.sort(
        (keys, jnp.concatenate([jnp.arange(n_tok, dtype=jnp.int32), zi]),
         jnp.concatenate([jnp.ones((n_tok,), jnp.int32), zi]),
         jnp.concatenate([rt[1], zf]), jnp.concatenate([rt[2], zf])),
        num_keys=1, is_stable=True)
    rows = jnp.arange(r, dtype=jnp.int32)[None, :]
    spare = jnp.broadcast_to(n_tok + rows, (nt, r))
    sidx = jnp.concatenate([spare[:1], jnp.where(valid.reshape(nt, r) == 1, gidx.reshape(nt, r), spare)], axis=0)
    gates = jnp.stack([gate_a, gate_b], axis=-1)
    tix = jnp.minimum(jnp.arange(nt, dtype=jnp.int32), nvalid - 1)
    tb = jnp.minimum(jnp.sum((tix[:, None] >= tile_end[None, :]).astype(jnp.int32), axis=1), N_BUCKETS - 1)
    grp = tb // N_PAIRS
    ea = (grp * EXPERTS_PER_GROUP + jnp.asarray(PAIR_A, jnp.int32)[tb % N_PAIRS]).astype(jnp.int32)
    eb = (grp * EXPERTS_PER_GROUP + jnp.asarray(PAIR_B, jnp.int32)[tb % N_PAIRS]).astype(jnp.int32)
    nv = jnp.concatenate([nvalid.reshape(1), jnp.sum(valid.reshape(nt, r), axis=1)]).astype(jnp.int32)
    gidx3 = gidx.reshape(nt, 1, r)
    sidx3 = sidx.reshape(nt + 1, 1, r)

    def smem(index):
        return pl.BlockSpec((1, 1, r), lambda t, ea, eb, nv: (index(t), 0, 0), memory_space=pltpu.SMEM)

    w_in_a = pl.BlockSpec((1, 1, D_MODEL, D_EXPERT), lambda t, ea, eb, nv: (layer, ea[t], 0, 0))
    w_out_a = pl.BlockSpec((1, 1, D_EXPERT, D_MODEL), lambda t, ea, eb, nv: (layer, ea[t], 0, 0))
    w_in_b = pl.BlockSpec((1, 1, D_MODEL, D_EXPERT), lambda t, ea, eb, nv: (layer, eb[t], 0, 0))
    w_out_b = pl.BlockSpec((1, 1, D_EXPERT, D_MODEL), lambda t, ea, eb, nv: (layer, eb[t], 0, 0))
    grid_spec = pltpu.PrefetchScalarGridSpec(
        num_scalar_prefetch=3,
        grid=(nt,),
        in_specs=[smem(lambda t: t), smem(lambda t: jnp.minimum(t + 1, nt - 1)),
                  smem(lambda t: t), smem(lambda t: t + 1),
                  pl.BlockSpec((r, 2), lambda t, ea, eb, nv: (t, 0)),
                  pl.BlockSpec(memory_space=pl.ANY),
                  w_in_a, w_in_a, w_out_a, w_in_b, w_in_b, w_out_b],
        out_specs=pl.BlockSpec(memory_space=pl.ANY),
        scratch_shapes=[pltpu.VMEM((2, r * N_CHUNKS, LANES), F32),
                        pltpu.VMEM((2, r * N_CHUNKS, LANES), F32),
                        pltpu.SemaphoreType.DMA((2,)),
                        pltpu.SemaphoreType.DMA],
    )
    y = pl.pallas_call(
        functools.partial(_moe_kernel, r),
        grid_spec=grid_spec,
        out_shape=jax.ShapeDtypeStruct((n_tok + r, N_CHUNKS, LANES), F32),
        compiler_params=_cparams(1),
        name="moe_experts",
    )(ea, eb, nv, gidx3, gidx3, sidx3, sidx3, gates, h2_slab, wg, wu, wd, wg, wu, wd)
    return y


def _final_kernel(npt, x1_ref, y_ref, g2_ref, gf_ref, op_ref, os_ref):
    i = pl.program_id(0)
    x = x1_ref[...] + g2_ref[0] * _slab_to_nat(y_ref, TM)
    ms = jnp.mean(x * x, axis=-1, keepdims=True)
    out = x * lax.rsqrt(ms + EPS) * gf_ref[...]

    @pl.when(i < npt)
    def _():
        op_ref[...] = out

    @pl.when(i >= npt)
    def _():
        os_ref[...] = out


def _rope_tables(dseq):
    n_freq = HEAD_DIM // 4
    inv = ROPE_THETA ** (-jnp.arange(0, 2 * n_freq, 2, dtype=F32) / (2 * n_freq))
    pos = jnp.arange(dseq)
    row = (pos // GRID_W).astype(F32)
    col = (pos % GRID_W).astype(F32)
    ang_r = row[:, None] * inv
    ang_c = col[:, None] * inv
    ang = jnp.concatenate([ang_r, ang_r, ang_c, ang_c], axis=-1)
    sign = jnp.tile(jnp.concatenate([-jnp.ones((n_freq,), F32), jnp.ones((n_freq,), F32)]), 2)
    cos = jnp.tile(jnp.cos(ang), (1, 2))
    sin = jnp.tile(jnp.sin(ang) * sign, (1, 2))
    cos = jnp.concatenate([jnp.ones((TM, LANES), F32), cos], axis=0)
    sin = jnp.concatenate([jnp.zeros((TM, LANES), F32), sin], axis=0)
    return cos, sin


def _dup_halves(w, n_heads):
    lead = w.shape[:-1]
    w = w.reshape(lead + (n_heads, 1, HEAD_DIM))
    w = jnp.broadcast_to(w, lead + (n_heads, 2, HEAD_DIM))
    return w.reshape(lead + (n_heads * 2 * HEAD_DIM,))


def kernel(x_prompt, x_sample, cache_k_a, cache_v_a, cache_k_b, cache_v_b, c, c_ctx, w_mod, b_mod, g_attn, g_ffn, w_qkv_a, w_o_a, g_q_a, g_k_a, w_qkv_b, w_o_b, lam_q1, lam_k1, lam_q2, lam_k2, g_sub_b, w_router, b_router, w_gate, w_up, w_down, g_final):
    nbp, seq, d = x_prompt.shape
    nbs, dseq, _ = x_sample.shape
    depth = w_mod.shape[0]
    assert d == D_MODEL and seq % TM == 0 and dseq % TM == 0
    tp, ts = nbp * seq, nbs * dseq
    t_all = tp + ts
    npt = tp // TM
    tpb = dseq // TM
    nti = t_all // TM
    assert nbs + 1 <= N_MOD_ROWS

    xp = x_prompt.reshape(tp, d)
    xs = x_sample.reshape(ts, d)

    cvec = jnp.zeros((N_MOD_ROWS, d), F32).at[0].set(c_ctx).at[1:1 + nbs].set(c)
    mod = _modulation(cvec, w_mod, b_mod).reshape(depth * N_MOD_ROWS * N_MOD, 1, d)

    cos_t, sin_t = _rope_tables(dseq)
    qscale = HEAD_DIM ** -0.5 * math.log2(math.e)

    nqk = D_MODEL + 2 * A_KV_HEADS * HEAD_DIM
    lane_id = jnp.arange(nqk)
    seg = jnp.where(lane_id < D_MODEL, lane_id // HEAD_DIM, A_HEADS + (lane_id - D_MODEL) // (2 * HEAD_DIM))
    seg_w = jnp.where(lane_id < D_MODEL, 1.0 / HEAD_DIM, 1.0 / (2 * HEAD_DIM))
    onehot = (seg[:, None] == jnp.arange(LANES)[None, :]).astype(F32)
    eq = (onehot * seg_w[:, None]).astype(BF16)
    et = jnp.concatenate([onehot.T, onehot.T], axis=0).astype(BF16)

    w_router_p = jnp.zeros((d, LANES), F32).at[:, :N_EXPERTS].set(w_router)
    wr_hi = w_router_p.astype(BF16)
    wr_lo = (w_router_p - wr_hi.astype(F32)).astype(BF16)
    br = jnp.zeros((1, LANES), F32).at[0, :N_EXPERTS].set(b_router)

    past = cache_k_a.shape[2]
    ck_a = _dup_halves(cache_k_a.reshape(nbs, -1, past, A_KV_HEADS * HEAD_DIM), A_KV_HEADS).astype(BF16)
    cv_a = _dup_halves(cache_v_a.reshape(nbs, -1, past, A_KV_HEADS * HEAD_DIM), A_KV_HEADS).astype(BF16)
    ck_b = cache_k_b.reshape(nbs, -1, past, D_MODEL).astype(BF16)
    cv_b = cache_v_b.reshape(nbs, -1, past, D_MODEL).astype(BF16)

    tok_spec = pl.BlockSpec((TM, D_MODEL), lambda i: (i, 0))
    tiles_per_seq = seq // TM

    def state_spec(w, j):
        def index(i):
            ii = jnp.minimum(i, npt - 1)
            return (ii // tiles_per_seq, j, ii % tiles_per_seq, 0)
        return pl.BlockSpec((1, 1, TM, w), index)

    any_spec = pl.BlockSpec(memory_space=pl.ANY)
    n_a, n_b = (depth + 1) // 2, depth // 2
    wkv = 2 * A_KV_HEADS * HEAD_DIM
    state_k_a = jnp.zeros((nbp, n_a, seq, wkv // 2), F32)
    state_v_a = jnp.zeros((nbp, n_a, seq, wkv // 2), F32)
    state_k_b = jnp.zeros((nbp, n_b, seq, d), F32)
    state_v_b = jnp.zeros((nbp, n_b, seq, d), F32)
    prev = None
    for l in range(depth):
        j = l // 2
        has_prev = prev is not None
        if has_prev:
            x1_prev, y_prev, l_prev = prev
            x_args = [x1_prev, y_prev, mod]
            x_specs = _x_specs(True, npt)
            x_specs[2] = _mod_spec(l_prev, 5, npt, tpb)
        else:
            x_args = [xp, xs]
            x_specs = _x_specs(False, npt)
        mod_args = lambda ks: [mod] * len(ks)
        mod_specs = lambda ks: [_mod_spec(l, k, npt, tpb) for k in ks]
        g_attn_l = g_attn[l].reshape(1, d)

        if l % 2 == 0:
            wq, wk, wv = jnp.split(w_qkv_a[j], [A_HEADS * HEAD_DIM, (A_HEADS + A_KV_HEADS) * HEAD_DIM], axis=-1)
            w_all = jnp.concatenate([wq, _dup_halves(wk, A_KV_HEADS), _dup_halves(wv, A_KV_HEADS)], axis=-1).astype(BF16)
            gvec = jnp.concatenate([jnp.tile(g_q_a[j], A_HEADS) * qscale,
                                    jnp.tile(g_k_a[j], 2 * A_KV_HEADS)]).reshape(1, nqk)
            n_in = len(x_specs) + 11
            q, k, v, state_k_a, state_v_a = pl.pallas_call(
                functools.partial(_proj_a_kernel, has_prev, npt),
                grid=(nti,),
                in_specs=x_specs + mod_specs((0, 1)) + [
                    _const_spec((1, d)), _const_spec(w_all.shape), _const_spec(eq.shape),
                    _const_spec(et.shape), _const_spec((1, nqk)), _rope_spec(npt, tpb), _rope_spec(npt, tpb),
                    any_spec, any_spec],
                out_specs=[tok_spec, pl.BlockSpec((TM, wkv), lambda i: (i, 0)),
                           pl.BlockSpec((TM, wkv), lambda i: (i, 0)),
                           state_spec(wkv // 2, j), state_spec(wkv // 2, j)],
                out_shape=[jax.ShapeDtypeStruct((t_all, d), BF16),
                           jax.ShapeDtypeStruct((t_all, wkv), BF16),
                           jax.ShapeDtypeStruct((t_all, wkv), BF16),
                           jax.ShapeDtypeStruct(state_k_a.shape, F32),
                           jax.ShapeDtypeStruct(state_v_a.shape, F32)],
                input_output_aliases={n_in - 2: 3, n_in - 1: 4},
                compiler_params=_cparams(1),
                name="proj_a",
            )(*x_args, *mod_args((0, 1)), g_attn_l, w_all, eq, et, gvec, cos_t, sin_t, state_k_a, state_v_a)
            o_p, o_s = _attention("a", TQ_A, q, k, v, nbp, seq, nbs, dseq, ck_a, cv_a, j)
            w_o = w_o_a[j].astype(BF16)
        else:
            lam_init = 0.8 - 0.6 * math.exp(-0.3 * l)
            lam = (jnp.exp(jnp.sum((lam_q1[j] * lam_k1[j]).astype(F32)))
                   - jnp.exp(jnp.sum((lam_q2[j] * lam_k2[j]).astype(F32))) + lam_init).reshape(1, 1)
            wq, wk, wv = jnp.split(w_qkv_b[j], 3, axis=-1)
            w_all = jnp.concatenate([wq * qscale, wk, wv], axis=-1).astype(BF16)
            n_in = len(x_specs) + 8
            q, k, v, state_k_b, state_v_b = pl.pallas_call(
                functools.partial(_proj_b_kernel, has_prev, npt),
                grid=(nti,),
                in_specs=x_specs + mod_specs((0, 1)) + [
                    _const_spec((1, d)), _const_spec(w_all.shape), _rope_spec(npt, tpb), _rope_spec(npt, tpb),
                    any_spec, any_spec],
                out_specs=[tok_spec, tok_spec, tok_spec, state_spec(d, j), state_spec(d, j)],
                out_shape=[jax.ShapeDtypeStruct((t_all, d), BF16),
                           jax.ShapeDtypeStruct((t_all, d), BF16),
                           jax.ShapeDtypeStruct((t_all, d), BF16),
                           jax.ShapeDtypeStruct(state_k_b.shape, F32),
                           jax.ShapeDtypeStruct(state_v_b.shape, F32)],
                input_output_aliases={n_in - 2: 3, n_in - 1: 4},
                compiler_params=_cparams(1),
                name="proj_b",
            )(*x_args, *mod_args((0, 1)), g_attn_l, w_all, cos_t, sin_t, state_k_b, state_v_b)
            gsub = (g_sub_b[j] * (1.0 - lam_init)).reshape(1, 2 * HEAD_DIM)
            o_p, o_s = _attention("b", TQ_B, q, k, v, nbp, seq, nbs, dseq, ck_b, cv_b, j, lam=lam, gsub=gsub)
            w_o = w_o_b[j].astype(BF16)

        x1, h2, rt = pl.pallas_call(
            functools.partial(_kb_kernel, has_prev, npt),
            grid=(nti,),
            in_specs=x_specs + [
                pl.BlockSpec((TM, d), lambda i: (jnp.minimum(i, npt - 1), 0)),
                pl.BlockSpec((TM, d), lambda i: (jnp.maximum(i - npt, 0), 0))] + mod_specs((2, 3, 4)) + [
                _const_spec((d, d)), _const_spec((1, d)), _const_spec((d, LANES)), _const_spec((d, LANES)),
                _const_spec((1, LANES))],
            out_specs=[tok_spec, pl.BlockSpec((TM * N_CHUNKS, LANES), lambda i: (i, 0)),
                       pl.BlockSpec((8, TM), lambda i: (0, i))],
            out_shape=[jax.ShapeDtypeStruct((t_all, d), F32),
                       jax.ShapeDtypeStruct((t_all * N_CHUNKS, LANES), F32),
                       jax.ShapeDtypeStruct((8, t_all), F32)],
            scratch_shapes=[pltpu.VMEM((LANES, TM), F32)],
            compiler_params=_cparams(1),
            name="oproj_router",
        )(*x_args, o_p, o_s, *mod_args((2, 3, 4)), w_o, g_ffn[l].reshape(1, d), wr_hi, wr_lo, br)

        y = _moe(h2.reshape(t_all, N_CHUNKS, LANES), rt, w_gate, w_up, w_down, l, t_all)
        prev = (x1, y.reshape((t_all + R_MOE) * N_CHUNKS, LANES), l)

    x1_prev, y_prev, l_prev = prev
    y_p, y_s = pl.pallas_call(
        functools.partial(_final_kernel, npt),
        grid=(nti,),
        in_specs=[tok_spec, pl.BlockSpec((TM * N_CHUNKS, LANES), lambda i: (i, 0)),
                  _mod_spec(l_prev, 5, npt, tpb), _const_spec((1, d))],
        out_specs=[pl.BlockSpec((TM, d), lambda i: (jnp.minimum(i, npt - 1), 0)),
                   pl.BlockSpec((TM, d), lambda i: (jnp.maximum(i - npt, 0), 0))],
        out_shape=[jax.ShapeDtypeStruct((tp, d), F32), jax.ShapeDtypeStruct((ts, d), F32)],
        compiler_params=_cparams(1),
        name="final_norm",
    )(x1_prev, y_prev, mod, g_final.reshape(1, d))

    return (y_p.reshape(nbp, seq, d), y_s.reshape(nbs, dseq, d),
            state_k_a.reshape(nbp, n_a, seq, A_KV_HEADS, HEAD_DIM),
            state_v_a.reshape(nbp, n_a, seq, A_KV_HEADS, HEAD_DIM),
            state_k_b.reshape(nbp, n_b, seq, B_HEADS, 2, HEAD_DIM),
            state_v_b.reshape(nbp, n_b, seq, B_HEADS, 2 * HEAD_DIM))
```

```python
import functools
import math

import jax
import jax.numpy as jnp
from jax import lax
from jax.experimental import pallas as pl
from jax.experimental.pallas import tpu as pltpu

F32 = jnp.float32
BF16 = jnp.bfloat16

D_MODEL = 1024
HEAD_DIM = 64
A_HEADS = 16
A_KV_HEADS = 4
B_HEADS = 8
N_EXPERTS = 16
N_GROUPS = 4
EXPERTS_PER_GROUP = 4
D_EXPERT = 512
N_MOD = 6
N_MOD_ROWS = 16
ROPE_THETA = 10000.0
GRID_W = 64
EPS = 1e-6

LANES = 128
N_CHUNKS = D_MODEL // LANES
TM = 256
TQ_A = 256
TQ_B = 512
R_MOE = 256
COPY_GROUP = 32
N_PAIRS = 6
N_BUCKETS = N_GROUPS * N_PAIRS
PAIR_A = (0, 0, 0, 1, 1, 3)
PAIR_B = (1, 2, 3, 3, 2, 2)
VMEM_LIMIT = 56 * 1024 * 1024


def _cparams(n_axes=1):
    return pltpu.CompilerParams(dimension_semantics=("arbitrary",) * n_axes,
                                vmem_limit_bytes=VMEM_LIMIT)


def _slab_to_nat(ref, tm):
    return jnp.concatenate([ref[pl.ds(c, tm, stride=N_CHUNKS), :] for c in range(N_CHUNKS)], axis=-1)


def _nat_to_slab(ref, val, tm):
    for c in range(N_CHUNKS):
        ref[pl.ds(c, tm, stride=N_CHUNKS), :] = val[:, LANES * c:LANES * (c + 1)]


def _norm_mod(x, g, shift, scale):
    ms = jnp.mean(x * x, axis=-1, keepdims=True)
    return x * lax.rsqrt(ms + EPS) * g * (1.0 + scale) + shift


def _rope_cols(x, cos, sin, second_of_pair):
    outs = []
    for c in range(x.shape[1] // LANES):
        xc = x[:, LANES * c:LANES * (c + 1)]
        partner = jnp.where(second_of_pair, pltpu.roll(xc, 16, 1), pltpu.roll(xc, LANES - 16, 1))
        outs.append(xc * cos + partner * sin)
    return jnp.concatenate(outs, axis=-1)


def _stream_x(i, npt, refs, has_prev, tm):
    if has_prev:
        x1_ref, y_ref, g2_ref = refs
        return x1_ref[...] + g2_ref[0] * _slab_to_nat(y_ref, tm)
    xp_ref, xs_ref = refs
    return jnp.where(i < npt, xp_ref[...], xs_ref[...])


def _mod_kernel(cv_ref, w_ref, b_ref, o_ref):
    cv = cv_ref[...]
    s = cv * (1.0 / (1.0 + jnp.exp(-cv)))
    o_ref[0] = jnp.dot(s.astype(BF16), w_ref[0].astype(BF16), preferred_element_type=F32) + b_ref[0]


def _modulation(cvec, w_mod, b_mod):
    depth, d, n = w_mod.shape
    tn = 1024
    return pl.pallas_call(
        _mod_kernel,
        grid=(depth, n // tn),
        in_specs=[pl.BlockSpec((N_MOD_ROWS, d), lambda l, j: (0, 0)),
                  pl.BlockSpec((1, d, tn), lambda l, j: (l, 0, j)),
                  pl.BlockSpec((1, 1, tn), lambda l, j: (l, 0, j))],
        out_specs=pl.BlockSpec((1, N_MOD_ROWS, tn), lambda l, j: (l, 0, j)),
        out_shape=jax.ShapeDtypeStruct((depth, N_MOD_ROWS, n), F32),
        compiler_params=_cparams(2),
        name="modulation",
    )(cvec, w_mod, b_mod.reshape(depth, 1, n))


def _proj_a_kernel(has_prev, npt, *refs):
    n_x = 3 if has_prev else 2
    xrefs, refs = refs[:n_x], refs[n_x:]
    (sh_ref, sc_ref, ga_ref, w_ref, eq_ref, et_ref, gv_ref, cos_ref, sin_ref, _, _,
     q_ref, k_ref, v_ref, ks_ref, vs_ref) = refs
    i = pl.program_id(0)
    x = _stream_x(i, npt, xrefs, has_prev, TM)
    hb = _norm_mod(x, ga_ref[...], sh_ref[0], sc_ref[0]).astype(BF16)
    qkv = jnp.dot(hb, w_ref[...], preferred_element_type=F32)
    nqk = D_MODEL + 2 * A_KV_HEADS * HEAD_DIM
    qk = qkv[:, :nqk]
    ms = jnp.dot((qk * qk).astype(BF16), eq_ref[...], preferred_element_type=F32)
    inv = lax.rsqrt(ms + EPS)
    inv_hi = inv.astype(BF16)
    inv_lo = (inv - inv_hi.astype(F32)).astype(BF16)
    invf = jnp.dot(jnp.concatenate([inv_hi, inv_lo], axis=-1), et_ref[...], preferred_element_type=F32)
    qkn = qk * invf * gv_ref[...]
    lane = lax.broadcasted_iota(jnp.int32, (1, LANES), 1)
    second = (lane & 16) != 0
    lo = lane < HEAD_DIM
    qkr = _rope_cols(qkn, cos_ref[...], sin_ref[...], second)
    q_ref[...] = qkr[:, :D_MODEL].astype(BF16)
    k_ref[...] = qkr[:, D_MODEL:].astype(BF16)
    vd = qkv[:, nqk:]
    v_ref[...] = vd.astype(BF16)

    @pl.when(i < npt)
    def _():
        kd = qkn[:, D_MODEL:]
        for j in range(2):
            ks_ref[0, 0, :, LANES * j:LANES * (j + 1)] = jnp.where(
                lo, kd[:, LANES * 2 * j:LANES * (2 * j + 1)], kd[:, LANES * (2 * j + 1):LANES * (2 * j + 2)])
            vs_ref[0, 0, :, LANES * j:LANES * (j + 1)] = jnp.where(
                lo, vd[:, LANES * 2 * j:LANES * (2 * j + 1)], vd[:, LANES * (2 * j + 1):LANES * (2 * j + 2)])


def _proj_b_kernel(has_prev, npt, *refs):
    n_x = 3 if has_prev else 2
    xrefs, refs = refs[:n_x], refs[n_x:]
    (sh_ref, sc_ref, ga_ref, w_ref, cos_ref, sin_ref, _, _,
     q_ref, k_ref, v_ref, ks_ref, vs_ref) = refs
    i = pl.program_id(0)
    x = _stream_x(i, npt, xrefs, has_prev, TM)
    hb = _norm_mod(x, ga_ref[...], sh_ref[0], sc_ref[0]).astype(BF16)
    qkv = jnp.dot(hb, w_ref[...], preferred_element_type=F32)
    lane = lax.broadcasted_iota(jnp.int32, (1, LANES), 1)
    second = (lane & 16) != 0
    qk = qkv[:, :2 * D_MODEL]
    qkr = _rope_cols(qk, cos_ref[...], sin_ref[...], second)
    q_ref[...] = qkr[:, :D_MODEL].astype(BF16)
    k_ref[...] = qkr[:, D_MODEL:].astype(BF16)
    v = qkv[:, 2 * D_MODEL:]
    v_ref[...] = v.astype(BF16)

    @pl.when(i < npt)
    def _():
        ks_ref[0, 0] = qk[:, D_MODEL:]
        vs_ref[0, 0] = v


def _x_specs(has_prev, npt):
    if has_prev:
        return [pl.BlockSpec((TM, D_MODEL), lambda i: (i, 0)),
                pl.BlockSpec((TM * N_CHUNKS, LANES), lambda i: (i, 0)),
                None]
    return [pl.BlockSpec((TM, D_MODEL), lambda i: (jnp.minimum(i, npt - 1), 0)),
            pl.BlockSpec((TM, D_MODEL), lambda i: (jnp.maximum(i - npt, 0), 0))]


def _mod_spec(layer, k, npt, tpb):
    def index(i):
        row = jnp.where(i < npt, 0, 1 + (i - npt) // tpb)
        return ((layer * N_MOD_ROWS + row) * N_MOD + k, 0, 0)
    return pl.BlockSpec((1, 1, D_MODEL), index)


def _rope_spec(npt, tpb):
    return pl.BlockSpec((TM, LANES), lambda i: (jnp.where(i < npt, 0, 1 + (i - npt) % tpb), 0))


def _const_spec(shape):
    nd = len(shape)
    return pl.BlockSpec(shape, lambda i: (0,) * nd)


def _lane_masks():
    lane = lax.broadcasted_iota(jnp.int32, (1, LANES), 1)
    lo = lane < HEAD_DIM
    mlo = jnp.where(lo, 1.0, 0.0).astype(BF16)
    mhi = jnp.where(lo, 0.0, 1.0).astype(BF16)
    return lo, mlo, mhi


_NT_DIMS = (((1,), (1,)), ((), ()))


def _scores_stage(lhs, cols, lk, has_cache, k_ref, ck_ref, s_scr, m_scr):
    s = lax.dot_general(lhs, k_ref[:, cols], _NT_DIMS, preferred_element_type=F32)
    s_scr[:, 0:lk] = s
    mp = s[:, 0:LANES]
    for c in range(1, lk // LANES):
        mp = jnp.maximum(mp, s[:, LANES * c:LANES * (c + 1)])
    if has_cache:
        s2 = lax.dot_general(lhs, ck_ref[0, 0, :, cols], _NT_DIMS, preferred_element_type=F32)
        past = s2.shape[1]
        s_scr[:, lk:lk + past] = s2
        for c in range(past // LANES):
            mp = jnp.maximum(mp, s2[:, LANES * c:LANES * (c + 1)])
    m_scr[...] = mp


def _weighted_values(s_scr, m_scr, lk, has_cache, v_ref, cv_ref, cols):
    m = jnp.max(m_scr[...], axis=-1, keepdims=True)
    pb = jnp.exp2(s_scr[...] - m).astype(BF16)
    rhs = jnp.concatenate([v_ref[:, cols], jnp.ones((lk, LANES), BF16)], axis=-1)
    res = jnp.dot(pb[:, 0:lk], rhs, preferred_element_type=F32)
    if has_cache:
        cv = cv_ref[0, 0, :, cols]
        rhs2 = jnp.concatenate([cv, jnp.ones(cv.shape, BF16)], axis=-1)
        res = res + jnp.dot(pb[:, lk:], rhs2, preferred_element_type=F32)
    return res


def _attn_a_kernel(tq, has_cache, *refs):
    if has_cache:
        q_ref, k_ref, v_ref, ck_ref, cv_ref, o_ref, s0, s1, m0, m1 = refs
    else:
        q_ref, k_ref, v_ref, o_ref, s0, s1, m0, m1 = refs
        ck_ref = cv_ref = None
    lk = k_ref.shape[0]
    lo, mlo, mhi = _lane_masks()
    bufs = ((s0, m0), (s1, m1))

    def scores(g, s_scr, m_scr):
        qa = q_ref[:, LANES * 2 * g:LANES * (2 * g + 1)]
        qb = q_ref[:, LANES * (2 * g + 1):LANES * (2 * g + 2)]
        lhs = jnp.concatenate([qa * mlo, qa * mhi, qb * mlo, qb * mhi], axis=0)
        _scores_stage(lhs, slice(LANES * g, LANES * (g + 1)), lk, has_cache, k_ref, ck_ref, s_scr, m_scr)

    def finish(g, s_scr, m_scr):
        cols = slice(LANES * g, LANES * (g + 1))
        res = _weighted_values(s_scr, m_scr, lk, has_cache, v_ref, cv_ref, cols)
        o = res[:, 0:LANES] * (1.0 / res[:, LANES:])
        o_ref[:, LANES * 2 * g:LANES * (2 * g + 1)] = jnp.where(lo, o[0:tq], o[tq:2 * tq]).astype(BF16)
        o_ref[:, LANES * (2 * g + 1):LANES * (2 * g + 2)] = jnp.where(
            lo, o[2 * tq:3 * tq], o[3 * tq:4 * tq]).astype(BF16)

    scores(0, *bufs[0])
    for g in range(A_KV_HEADS):
        if g + 1 < A_KV_HEADS:
            scores(g + 1, *bufs[(g + 1) % 2])
        finish(g, *bufs[g % 2])


def _attn_b_kernel(tq, has_cache, *refs):
    if has_cache:
        lam_ref, q_ref, k_ref, v_ref, ck_ref, cv_ref, gs_ref, o_ref, s0, s1, m0, m1 = refs
    else:
        lam_ref, q_ref, k_ref, v_ref, gs_ref, o_ref, s0, s1, m0, m1 = refs
        ck_ref = cv_ref = None
    lk = k_ref.shape[0]
    lam = lam_ref[0, 0]
    _, mlo, mhi = _lane_masks()
    bufs = ((s0, m0), (s1, m1))

    def scores(h, s_scr, m_scr):
        cols = slice(LANES * h, LANES * (h + 1))
        qc = q_ref[:, cols]
        lhs = jnp.concatenate([qc * mlo, qc * mhi], axis=0)
        _scores_stage(lhs, cols, lk, has_cache, k_ref, ck_ref, s_scr, m_scr)

    def finish(h, s_scr, m_scr):
        cols = slice(LANES * h, LANES * (h + 1))
        res = _weighted_values(s_scr, m_scr, lk, has_cache, v_ref, cv_ref, cols)
        on = res[:, 0:LANES] * (1.0 / res[:, LANES:])
        o = on[0:tq] - lam * on[tq:2 * tq]
        ms = jnp.mean(o * o, axis=-1, keepdims=True)
        o_ref[:, cols] = (o * lax.rsqrt(ms + EPS) * gs_ref[...]).astype(BF16)

    scores(0, *bufs[0])
    for h in range(B_HEADS):
        if h + 1 < B_HEADS:
            scores(h + 1, *bufs[(h + 1) % 2])
        finish(h, *bufs[h % 2])


def _attention(kind, tq, q, k, v, nbp, seq, nbs, dseq, cache_k, cache_v, layer_j, lam=None, gsub=None):
    t_all = q.shape[0]
    wk = k.shape[1]
    kern = _attn_a_kernel if kind == "a" else _attn_b_kernel
    pre_args, pre_specs_1, pre_specs_2 = [], [], []
    post_args, post_specs_1, post_specs_2 = [], [], []
    if kind == "b":
        pre_args = [lam]
        pre_specs_1 = [pl.BlockSpec(memory_space=pltpu.SMEM)]
        pre_specs_2 = [pl.BlockSpec(memory_space=pltpu.SMEM)]
        post_args = [gsub]
        post_specs_1 = [pl.BlockSpec((1, LANES), lambda b: (0, 0))]
        post_specs_2 = [pl.BlockSpec((1, LANES), lambda b, j: (0, 0))]

    stack = 4 if kind == "a" else 2

    def scratch(rows, keys):
        return [pltpu.VMEM((rows, keys), F32), pltpu.VMEM((rows, keys), F32),
                pltpu.VMEM((rows, LANES), F32), pltpu.VMEM((rows, LANES), F32)]

    tqp = min(tq, seq)
    nqp = seq // tqp
    out_p = pl.pallas_call(
        functools.partial(kern, tqp, False),
        grid=(nbp * nqp,),
        in_specs=pre_specs_1 + [
            pl.BlockSpec((tqp, D_MODEL), lambda b: (b, 0)),
            pl.BlockSpec((seq, wk), lambda b: (b // nqp, 0)),
            pl.BlockSpec((seq, wk), lambda b: (b // nqp, 0))] + post_specs_1,
        out_specs=pl.BlockSpec((tqp, D_MODEL), lambda b: (b, 0)),
        out_shape=jax.ShapeDtypeStruct((nbp * seq, D_MODEL), BF16),
        scratch_shapes=scratch(stack * tqp, seq),
        compiler_params=_cparams(1),
        name="attn_%s_ctx" % kind,
    )(*pre_args, q, k, v, *post_args)

    nq = dseq // tq
    qoff = (nbp * seq) // tq
    koff = (nbp * seq) // dseq
    assert (nbp * seq) % dseq == 0
    past = cache_k.shape[2]
    out_s = pl.pallas_call(
        functools.partial(kern, tq, True),
        grid=(nbs, nq),
        in_specs=pre_specs_2 + [
            pl.BlockSpec((tq, D_MODEL), lambda b, j: (qoff + b * nq + j, 0)),
            pl.BlockSpec((dseq, wk), lambda b, j: (koff + b, 0)),
            pl.BlockSpec((dseq, wk), lambda b, j: (koff + b, 0)),
            pl.BlockSpec((1, 1, past, wk), lambda b, j: (b, layer_j, 0, 0)),
            pl.BlockSpec((1, 1, past, wk), lambda b, j: (b, layer_j, 0, 0))] + post_specs_2,
        out_specs=pl.BlockSpec((tq, D_MODEL), lambda b, j: (b * nq + j, 0)),
        out_shape=jax.ShapeDtypeStruct((nbs * dseq, D_MODEL), BF16),
        scratch_shapes=scratch(stack * tq, dseq + past),
        compiler_params=_cparams(2),
        name="attn_%s_lat" % kind,
    )(*pre_args, q, k, v, cache_k, cache_v, *post_args)
    del t_all
    return out_p, out_s


def _first_max_index(vals, m):
    idx = jnp.full_like(m, float(len(vals) - 1))
    for j in range(len(vals) - 2, -1, -1):
        idx = jnp.where(vals[j] == m, float(j), idx)
    return idx


def _select(idx, vals):
    out = vals[-1]
    for j in range(len(vals) - 2, -1, -1):
        out = jnp.where(idx == float(j), vals[j], out)
    return out


def _kb_kernel(has_prev, npt, *refs):
    n_x = 3 if has_prev else 2
    xrefs, refs = refs[:n_x], refs[n_x:]
    (op_ref, os_ref, g1_ref, sh_ref, sc_ref, wo_ref, gf_ref, wrh_ref, wrl_ref, br_ref,
     x1_ref, h2_ref, rt_ref, lt_scr) = refs
    i = pl.program_id(0)
    x = _stream_x(i, npt, xrefs, has_prev, TM)
    o = jnp.where(i < npt, op_ref[...], os_ref[...])
    a = jnp.dot(o, wo_ref[...], preferred_element_type=F32)
    x1 = x + g1_ref[0] * a
    x1_ref[...] = x1
    h2 = _norm_mod(x1, gf_ref[...], sh_ref[0], sc_ref[0])
    _nat_to_slab(h2_ref, h2, TM)
    h_hi = h2.astype(BF16)
    h_lo = (h2 - h_hi.astype(F32)).astype(BF16)
    logits = (jnp.dot(h_hi, wrh_ref[...], preferred_element_type=F32)
              + jnp.dot(h_hi, wrl_ref[...], preferred_element_type=F32)
              + jnp.dot(h_lo, wrh_ref[...], preferred_element_type=F32)
              + br_ref[...])
    lt_scr[...] = logits.T
    sc = []
    for e in range(N_EXPERTS):
        z = lt_scr[e:e + 1, :]
        sc.append(1.0 / (1.0 + jnp.exp(-z)))
    ninf = jnp.float32(-jnp.inf)
    gscore, gtop = [], []
    for g in range(N_GROUPS):
        a4 = sc[EXPERTS_PER_GROUP * g:EXPERTS_PER_GROUP * (g + 1)]
        m1 = jnp.maximum(jnp.maximum(a4[0], a4[1]), jnp.maximum(a4[2], a4[3]))
        i1 = _first_max_index(a4, m1)
        b4 = [jnp.where(i1 == float(j), ninf, a4[j]) for j in range(EXPERTS_PER_GROUP)]
        m2 = jnp.maximum(jnp.maximum(b4[0], b4[1]), jnp.maximum(b4[2], b4[3]))
        i2 = _first_max_index(b4, m2)
        gscore.append(m1 + m2)
        gtop.append((m1, i1, m2, i2))
    gm = jnp.maximum(jnp.maximum(gscore[0], gscore[1]), jnp.maximum(gscore[2], gscore[3]))
    best = _first_max_index(gscore, gm)
    m1 = _select(best, [t[0] for t in gtop])
    i1 = _select(best, [t[1] for t in gtop])
    m2 = _select(best, [t[2] for t in gtop])
    i2 = _select(best, [t[3] for t in gtop])
    den = m1 + m2
    first_low = i1 < i2
    ia = jnp.minimum(i1, i2)
    ib = jnp.maximum(i1, i2)
    pid = jnp.where(ia == 0.0, ib - 1.0, jnp.where(ia == 1.0, 6.0 - ib, 5.0))
    g_low = jnp.where(first_low, m1, m2) / den
    g_high = jnp.where(first_low, m2, m1) / den
    swapped = pid == 5.0
    rt_ref[...] = jnp.zeros(rt_ref.shape, F32)
    rt_ref[0:1, :] = best * float(N_PAIRS) + pid
    rt_ref[1:2, :] = jnp.where(swapped, g_high, g_low)
    rt_ref[2:3, :] = jnp.where(swapped, g_low, g_high)


def _moe_kernel(r_tile, ea_ref, eb_ref, nv_ref,
                gcur_ref, gnext_ref, sprev_ref, scur_ref, gates_ref, h2_hbm,
                wga_ref, wua_ref, wda_ref, wgb_ref, wub_ref, wdb_ref,
                y_hbm, xbuf, ybuf, gsem, ssem):
    t = pl.program_id(0)
    nt = pl.num_programs(0)
    nvalid = nv_ref[0]
    slot = t % 2
    n_cur = nv_ref[1 + t]
    n_next = nv_ref[1 + jnp.minimum(t + 1, nt - 1)]
    n_scatter = jnp.where(t == 0, r_tile, nv_ref[jnp.maximum(t, 1)])
    n_scatter_before = jnp.where(t <= 1, r_tile, nv_ref[jnp.maximum(t - 1, 1)])

    def for_groups(n, fn):
        for c in range(r_tile // COPY_GROUP):
            pl.when(c * COPY_GROUP < n)(functools.partial(fn, c))

    def group_rows(buf, s, c):
        return buf.at[s, pl.ds(N_CHUNKS * COPY_GROUP * c, N_CHUNKS * COPY_GROUP)]

    def gather_start(idx_ref, s, n):
        def go(c):
            for r in range(COPY_GROUP * c, COPY_GROUP * (c + 1)):
                pltpu.make_async_copy(h2_hbm.at[idx_ref[0, 0, r]], xbuf.at[s, pl.ds(N_CHUNKS * r, N_CHUNKS)],
                                      gsem.at[s]).start()
        for_groups(n, go)

    def gather_wait(s, n):
        for_groups(n, lambda c: pltpu.make_async_copy(group_rows(xbuf, s, c), group_rows(xbuf, s, c),
                                                      gsem.at[s]).wait())

    def scatter_start(idx_ref, s, n):
        def go(c):
            for r in range(COPY_GROUP * c, COPY_GROUP * (c + 1)):
                pltpu.make_async_copy(ybuf.at[s, pl.ds(N_CHUNKS * r, N_CHUNKS)], y_hbm.at[idx_ref[0, 0, r]],
                                      ssem).start()
        for_groups(n, go)

    def scatter_wait(n):
        for_groups(n, lambda c: pltpu.make_async_copy(group_rows(ybuf, 0, c), group_rows(ybuf, 0, c),
                                                      ssem).wait())

    @pl.when(t == 0)
    def _():
        ybuf[1] = jnp.zeros(ybuf.shape[1:], F32)
        xbuf[...] = jnp.zeros(xbuf.shape, F32)
        gather_start(gcur_ref, 0, n_cur)

    @pl.when(t < nvalid)
    def _():
        @pl.when(t > 0)
        def _():
            scatter_wait(n_scatter_before)

        gather_wait(slot, n_cur)
        gather_start(gnext_ref, 1 - slot, n_next)
        scatter_start(sprev_ref, 1 - slot, n_scatter)
        x = _slab_to_nat(xbuf.at[slot], r_tile).astype(BF16)
        gts = gates_ref[...]
        y = None
        for (wg, wu, wd, col) in ((wga_ref, wua_ref, wda_ref, 0), (wgb_ref, wub_ref, wdb_ref, 1)):
            g = jnp.dot(x, wg[0, 0].astype(BF16), preferred_element_type=F32)
            u = jnp.dot(x, wu[0, 0].astype(BF16), preferred_element_type=F32)
            hmid = (g * (1.0 / (1.0 + jnp.exp(-g))) * u).astype(BF16)
            ye = gts[:, col:col + 1] * jnp.dot(hmid, wd[0, 0].astype(BF16), preferred_element_type=F32)
            y = ye if y is None else y + ye
        _nat_to_slab(ybuf.at[slot], y, r_tile)

        @pl.when(t == nvalid - 1)
        def _():
            gather_wait(1 - slot, n_next)
            scatter_wait(n_scatter)
            scatter_start(scur_ref, slot, n_cur)
            scatter_wait(n_cur)


def _moe(h2_slab, rt, wg, wu, wd, layer, n_tok):
    r = R_MOE
    nt = n_tok // r + N_BUCKETS
    bucket = rt[0].astype(jnp.int32)
    n_pad = nt * r - n_tok
    buckets = jnp.arange(N_BUCKETS, dtype=jnp.int32)
    counts = jnp.sum((bucket[:, None] == buckets[None, :]).astype(jnp.int32), axis=0)
    tiles_b = (counts + r - 1) // r
    tile_end = jnp.cumsum(tiles_b)
    nvalid = tile_end[-1]
    pad_end = jnp.cumsum(tiles_b * r - counts)
    pad_bucket = jnp.sum((jnp.arange(n_pad, dtype=jnp.int32)[:, None] >= pad_end[None, :]).astype(jnp.int32),
                         axis=1)
    keys = jnp.concatenate([2 * bucket, 2 * pad_bucket + 1])
    zi = jnp.zeros((n_pad,), jnp.int32)
    zf = jnp.zeros((n_pad,), F32)
    _, gidx, valid, gate_a, gate_b = lax.sort(
        (keys, jnp.concatenate([jnp.arange(n_tok, dtype=jnp.int32), jnp.arange(n_pad, dtype=jnp.int32) % n_tok]),
         jnp.concatenate([jnp.ones((n_tok,), jnp.int32), zi]),
         jnp.concatenate([rt[1], zf]), jnp.concatenate([rt[2], zf])),
        num_keys=1, is_stable=True)
    rows = jnp.arange(r, dtype=jnp.int32)[None, :]
    spare = jnp.broadcast_to(n_tok + rows, (nt, r))
    sidx = jnp.concatenate([spare[:1], jnp.where(valid.reshape(nt, r) == 1, gidx.reshape(nt, r), spare)], axis=0)
    gates = jnp.stack([gate_a, gate_b], axis=-1)
    tix = jnp.minimum(jnp.arange(nt, dtype=jnp.int32), nvalid - 1)
    tb = jnp.minimum(jnp.sum((tix[:, None] >= tile_end[None, :]).astype(jnp.int32), axis=1), N_BUCKETS - 1)
    grp = tb // N_PAIRS
    ea = (grp * EXPERTS_PER_GROUP + jnp.asarray(PAIR_A, jnp.int32)[tb % N_PAIRS]).astype(jnp.int32)
    eb = (grp * EXPERTS_PER_GROUP + jnp.asarray(PAIR_B, jnp.int32)[tb % N_PAIRS]).astype(jnp.int32)
    nv = jnp.concatenate([nvalid.reshape(1), jnp.sum(valid.reshape(nt, r), axis=1)]).astype(jnp.int32)
    gidx3 = gidx.reshape(nt, 1, r)
    sidx3 = sidx.reshape(nt + 1, 1, r)

    def smem(index):
        return pl.BlockSpec((1, 1, r), lambda t, ea, eb, nv: (index(t), 0, 0), memory_space=pltpu.SMEM)

    w_in_a = pl.BlockSpec((1, 1, D_MODEL, D_EXPERT), lambda t, ea, eb, nv: (layer, ea[t], 0, 0))
    w_out_a = pl.BlockSpec((1, 1, D_EXPERT, D_MODEL), lambda t, ea, eb, nv: (layer, ea[t], 0, 0))
    w_in_b = pl.BlockSpec((1, 1, D_MODEL, D_EXPERT), lambda t, ea, eb, nv: (layer, eb[t], 0, 0))
    w_out_b = pl.BlockSpec((1, 1, D_EXPERT, D_MODEL), lambda t, ea, eb, nv: (layer, eb[t], 0, 0))
    grid_spec = pltpu.PrefetchScalarGridSpec(
        num_scalar_prefetch=3,
        grid=(nt,),
        in_specs=[smem(lambda t: t), smem(lambda t: jnp.minimum(t + 1, nt - 1)),
                  smem(lambda t: t), smem(lambda t: t + 1),
                  pl.BlockSpec((r, 2), lambda t, ea, eb, nv: (t, 0)),
                  pl.BlockSpec(memory_space=pl.ANY),
                  w_in_a, w_in_a, w_out_a, w_in_b, w_in_b, w_out_b],
        out_specs=pl.BlockSpec(memory_space=pl.ANY),
        scratch_shapes=[pltpu.VMEM((2, r * N_CHUNKS, LANES), F32),
                        pltpu.VMEM((2, r * N_CHUNKS, LANES), F32),
                        pltpu.SemaphoreType.DMA((2,)),
                        pltpu.SemaphoreType.DMA],
    )
    y = pl.pallas_call(
        functools.partial(_moe_kernel, r),
        grid_spec=grid_spec,
        out_shape=jax.ShapeDtypeStruct((n_tok + r, N_CHUNKS, LANES), F32),
        compiler_params=_cparams(1),
        name="moe_experts",
    )(ea, eb, nv, gidx3, gidx3, sidx3, sidx3, gates, h2_slab, wg, wu, wd, wg, wu, wd)
    return y


def _final_kernel(npt, x1_ref, y_ref, g2_ref, gf_ref, op_ref, os_ref):
    i = pl.program_id(0)
    x = x1_ref[...] + g2_ref[0] * _slab_to_nat(y_ref, TM)
    ms = jnp.mean(x * x, axis=-1, keepdims=True)
    out = x * lax.rsqrt(ms + EPS) * gf_ref[...]

    @pl.when(i < npt)
    def _():
        op_ref[...] = out

    @pl.when(i >= npt)
    def _():
        os_ref[...] = out


def _rope_tables(dseq):
    n_freq = HEAD_DIM // 4
    inv = ROPE_THETA ** (-jnp.arange(0, 2 * n_freq, 2, dtype=F32) / (2 * n_freq))
    pos = jnp.arange(dseq)
    row = (pos // GRID_W).astype(F32)
    col = (pos % GRID_W).astype(F32)
    ang_r = row[:, None] * inv
    ang_c = col[:, None] * inv
    ang = jnp.concatenate([ang_r, ang_r, ang_c, ang_c], axis=-1)
    sign = jnp.tile(jnp.concatenate([-jnp.ones((n_freq,), F32), jnp.ones((n_freq,), F32)]), 2)
    cos = jnp.tile(jnp.cos(ang), (1, 2))
    sin = jnp.tile(jnp.sin(ang) * sign, (1, 2))
    cos = jnp.concatenate([jnp.ones((TM, LANES), F32), cos], axis=0)
    sin = jnp.concatenate([jnp.zeros((TM, LANES), F32), sin], axis=0)
    return cos, sin


def _dup_halves(w, n_heads):
    lead = w.shape[:-1]
    w = w.reshape(lead + (n_heads, 1, HEAD_DIM))
    w = jnp.broadcast_to(w, lead + (n_heads, 2, HEAD_DIM))
    return w.reshape(lead + (n_heads * 2 * HEAD_DIM,))


def kernel(x_prompt, x_sample, cache_k_a, cache_v_a, cache_k_b, cache_v_b, c, c_ctx, w_mod, b_mod, g_attn, g_ffn, w_qkv_a, w_o_a, g_q_a, g_k_a, w_qkv_b, w_o_b, lam_q1, lam_k1, lam_q2, lam_k2, g_sub_b, w_router, b_router, w_gate, w_up, w_down, g_final):
    nbp, seq, d = x_prompt.shape
    nbs, dseq, _ = x_sample.shape
    depth = w_mod.shape[0]
    assert d == D_MODEL and seq % TM == 0 and dseq % TM == 0
    tp, ts = nbp * seq, nbs * dseq
    t_all = tp + ts
    npt = tp // TM
    tpb = dseq // TM
    nti = t_all // TM
    assert nbs + 1 <= N_MOD_ROWS

    xp = x_prompt.reshape(tp, d)
    xs = x_sample.reshape(ts, d)

    cvec = jnp.zeros((N_MOD_ROWS, d), F32).at[0].set(c_ctx).at[1:1 + nbs].set(c)
    mod = _modulation(cvec, w_mod, b_mod).reshape(depth * N_MOD_ROWS * N_MOD, 1, d)

    cos_t, sin_t = _rope_tables(dseq)
    qscale = HEAD_DIM ** -0.5 * math.log2(math.e)

    nqk = D_MODEL + 2 * A_KV_HEADS * HEAD_DIM
    lane_id = jnp.arange(nqk)
    seg = jnp.where(lane_id < D_MODEL, lane_id // HEAD_DIM, A_HEADS + (lane_id - D_MODEL) // (2 * HEAD_DIM))
    seg_w = jnp.where(lane_id < D_MODEL, 1.0 / HEAD_DIM, 1.0 / (2 * HEAD_DIM))
    onehot = (seg[:, None] == jnp.arange(LANES)[None, :]).astype(F32)
    eq = (onehot * seg_w[:, None]).astype(BF16)
    et = jnp.concatenate([onehot.T, onehot.T], axis=0).astype(BF16)

    w_router_p = jnp.zeros((d, LANES), F32).at[:, :N_EXPERTS].set(w_router)
    wr_hi = w_router_p.astype(BF16)
    wr_lo = (w_router_p - wr_hi.astype(F32)).astype(BF16)
    br = jnp.zeros((1, LANES), F32).at[0, :N_EXPERTS].set(b_router)

    past = cache_k_a.shape[2]
    ck_a = _dup_halves(cache_k_a.reshape(nbs, -1, past, A_KV_HEADS * HEAD_DIM), A_KV_HEADS).astype(BF16)
    cv_a = _dup_halves(cache_v_a.reshape(nbs, -1, past, A_KV_HEADS * HEAD_DIM), A_KV_HEADS).astype(BF16)
    ck_b = cache_k_b.reshape(nbs, -1, past, D_MODEL).astype(BF16)
    cv_b = cache_v_b.reshape(nbs, -1, past, D_MODEL).astype(BF16)

    tok_spec = pl.BlockSpec((TM, D_MODEL), lambda i: (i, 0))
    tiles_per_seq = seq // TM

    def state_spec(w, j):
        def index(i):
            ii = jnp.minimum(i, npt - 1)
            return (ii // tiles_per_seq, j, ii % tiles_per_seq, 0)
        return pl.BlockSpec((1, 1, TM, w), index)

    any_spec = pl.BlockSpec(memory_space=pl.ANY)
    n_a, n_b = (depth + 1) // 2, depth // 2
    wkv = 2 * A_KV_HEADS * HEAD_DIM
    state_k_a = jnp.zeros((nbp, n_a, seq, wkv // 2), F32)
    state_v_a = jnp.zeros((nbp, n_a, seq, wkv // 2), F32)
    state_k_b = jnp.zeros((nbp, n_b, seq, d), F32)
    state_v_b = jnp.zeros((nbp, n_b, seq, d), F32)
    prev = None
    for l in range(depth):
        j = l // 2
        has_prev = prev is not None
        if has_prev:
            x1_prev, y_prev, l_prev = prev
            x_args = [x1_prev, y_prev, mod]
            x_specs = _x_specs(True, npt)
            x_specs[2] = _mod_spec(l_prev, 5, npt, tpb)
        else:
            x_args = [xp, xs]
            x_specs = _x_specs(False, npt)
        mod_args = lambda ks: [mod] * len(ks)
        mod_specs = lambda ks: [_mod_spec(l, k, npt, tpb) for k in ks]
        g_attn_l = g_attn[l].reshape(1, d)

        if l % 2 == 0:
            wq, wk, wv = jnp.split(w_qkv_a[j], [A_HEADS * HEAD_DIM, (A_HEADS + A_KV_HEADS) * HEAD_DIM], axis=-1)
            w_all = jnp.concatenate([wq, _dup_halves(wk, A_KV_HEADS), _dup_halves(wv, A_KV_HEADS)], axis=-1).astype(BF16)
            gvec = jnp.concatenate([jnp.tile(g_q_a[j], A_HEADS) * qscale,
                                    jnp.tile(g_k_a[j], 2 * A_KV_HEADS)]).reshape(1, nqk)
            n_in = len(x_specs) + 11
            q, k, v, state_k_a, state_v_a = pl.pallas_call(
                functools.partial(_proj_a_kernel, has_prev, npt),
                grid=(nti,),
                in_specs=x_specs + mod_specs((0, 1)) + [
                    _const_spec((1, d)), _const_spec(w_all.shape), _const_spec(eq.shape),
                    _const_spec(et.shape), _const_spec((1, nqk)), _rope_spec(npt, tpb), _rope_spec(npt, tpb),
                    any_spec, any_spec],
                out_specs=[tok_spec, pl.BlockSpec((TM, wkv), lambda i: (i, 0)),
                           pl.BlockSpec((TM, wkv), lambda i: (i, 0)),
                           state_spec(wkv // 2, j), state_spec(wkv // 2, j)],
                out_shape=[jax.ShapeDtypeStruct((t_all, d), BF16),
                           jax.ShapeDtypeStruct((t_all, wkv), BF16),
                           jax.ShapeDtypeStruct((t_all, wkv), BF16),
                           jax.ShapeDtypeStruct(state_k_a.shape, F32),
                           jax.ShapeDtypeStruct(state_v_a.shape, F32)],
                input_output_aliases={n_in - 2: 3, n_in - 1: 4},
                compiler_params=_cparams(1),
                name="proj_a",
            )(*x_args, *mod_args((0, 1)), g_attn_l, w_all, eq, et, gvec, cos_t, sin_t, state_k_a, state_v_a)
            o_p, o_s = _attention("a", TQ_A, q, k, v, nbp, seq, nbs, dseq, ck_a, cv_a, j)
            w_o = w_o_a[j].astype(BF16)
        else:
            lam_init = 0.8 - 0.6 * math.exp(-0.3 * l)
            lam = (jnp.exp(jnp.sum((lam_q1[j] * lam_k1[j]).astype(F32)))
                   - jnp.exp(jnp.sum((lam_q2[j] * lam_k2[j]).astype(F32))) + lam_init).reshape(1, 1)
            wq, wk, wv = jnp.split(w_qkv_b[j], 3, axis=-1)
            w_all = jnp.concatenate([wq * qscale, wk, wv], axis=-1).astype(BF16)
            n_in = len(x_specs) + 8
            q, k, v, state_k_b, state_v_b = pl.pallas_call(
                functools.partial(_proj_b_kernel, has_prev, npt),
                grid=(nti,),
                in_specs=x_specs + mod_specs((0, 1)) + [
                    _const_spec((1, d)), _const_spec(w_all.shape), _rope_spec(npt, tpb), _rope_spec(npt, tpb),
                    any_spec, any_spec],
                out_specs=[tok_spec, tok_spec, tok_spec, state_spec(d, j), state_spec(d, j)],
                out_shape=[jax.ShapeDtypeStruct((t_all, d), BF16),
                           jax.ShapeDtypeStruct((t_all, d), BF16),
                           jax.ShapeDtypeStruct((t_all, d), BF16),
                           jax.ShapeDtypeStruct(state_k_b.shape, F32),
                           jax.ShapeDtypeStruct(state_v_b.shape, F32)],
                input_output_aliases={n_in - 2: 3, n_in - 1: 4},
                compiler_params=_cparams(1),
                name="proj_b",
            )(*x_args, *mod_args((0, 1)), g_attn_l, w_all, cos_t, sin_t, state_k_b, state_v_b)
            gsub = (g_sub_b[j] * (1.0 - lam_init)).reshape(1, 2 * HEAD_DIM)
            o_p, o_s = _attention("b", TQ_B, q, k, v, nbp, seq, nbs, dseq, ck_b, cv_b, j, lam=lam, gsub=gsub)
            w_o = w_o_b[j].astype(BF16)

        x1, h2, rt = pl.pallas_call(
            functools.partial(_kb_kernel, has_prev, npt),
            grid=(nti,),
            in_specs=x_specs + [
                pl.BlockSpec((TM, d), lambda i: (jnp.minimum(i, npt - 1), 0)),
                pl.BlockSpec((TM, d), lambda i: (jnp.maximum(i - npt, 0), 0))] + mod_specs((2, 3, 4)) + [
                _const_spec((d, d)), _const_spec((1, d)), _const_spec((d, LANES)), _const_spec((d, LANES)),
                _const_spec((1, LANES))],
            out_specs=[tok_spec, pl.BlockSpec((TM * N_CHUNKS, LANES), lambda i: (i, 0)),
                       pl.BlockSpec((8, TM), lambda i: (0, i))],
            out_shape=[jax.ShapeDtypeStruct((t_all, d), F32),
                       jax.ShapeDtypeStruct((t_all * N_CHUNKS, LANES), F32),
                       jax.ShapeDtypeStruct((8, t_all), F32)],
            scratch_shapes=[pltpu.VMEM((LANES, TM), F32)],
            compiler_params=_cparams(1),
            name="oproj_router",
        )(*x_args, o_p, o_s, *mod_args((2, 3, 4)), w_o, g_ffn[l].reshape(1, d), wr_hi, wr_lo, br)

        y = _moe(h2.reshape(t_all, N_CHUNKS, LANES), rt, w_gate, w_up, w_down, l, t_all)
        prev = (x1, y.reshape((t_all + R_MOE) * N_CHUNKS, LANES), l)

    x1_prev, y_prev, l_prev = prev
    y_p, y_s = pl.pallas_call(
        functools.partial(_final_kernel, npt),
        grid=(nti,),
        in_specs=[tok_spec, pl.BlockSpec((TM * N_CHUNKS, LANES), lambda i: (i, 0)),
                  _mod_spec(l_prev, 5, npt, tpb), _const_spec((1, d))],
        out_specs=[pl.BlockSpec((TM, d), lambda i: (jnp.minimum(i, npt - 1), 0)),
                   pl.BlockSpec((TM, d), lambda i: (jnp.maximum(i - npt, 0), 0))],
        out_shape=[jax.ShapeDtypeStruct((tp, d), F32), jax.ShapeDtypeStruct((ts, d), F32)],
        compiler_params=_cparams(1),
        name="final_norm",
    )(x1_prev, y_prev, mod, g_final.reshape(1, d))

    return (y_p.reshape(nbp, seq, d), y_s.reshape(nbs, dseq, d),
            state_k_a.reshape(nbp, n_a, seq, A_KV_HEADS, HEAD_DIM),
            state_v_a.reshape(nbp, n_a, seq, A_KV_HEADS, HEAD_DIM),
            state_k_b.reshape(nbp, n_b, seq, B_HEADS, 2, HEAD_DIM),
            state_v_b.reshape(nbp, n_b, seq, B_HEADS, 2 * HEAD_DIM))
```
